```python
import math
import jax
import jax.numpy as jnp
from jax import lax
import numpy as np

D_MODEL = 2048
BATCH = 2
SEQ = 8192
DEPTH = 1
DEC_BATCH = 32
DEC_SEQ = 8
PAST_LEN = 16384
PAGE_SIZE = 128

MIX_WIDTH = D_MODEL
HEAD_DIM = 128
FOX_WIDTH = MIX_WIDTH // 2
DSA_WIDTH = MIX_WIDTH - FOX_WIDTH
FOX_HEADS = FOX_WIDTH // HEAD_DIM
DSA_HEADS = DSA_WIDTH // HEAD_DIM
IDX_HEADS = 16
IDX_DIM = 64
TOPK_MAX = 256
Q_BLOCK = 128
ROPE_THETA = 10000.0
LN_EPS = 1e-5
ATTN_SCALE = HEAD_DIM ** -0.5
IDX_W_SCALE = IDX_HEADS ** -0.5 * IDX_DIM ** -0.5
FORGET_BIAS_INIT = 2.0
DEEPNORM_ALPHA = (2.0 * DEPTH) ** 0.25
DEEPNORM_BETA = (8.0 * DEPTH) ** -0.25

WIDTHS = (FOX_WIDTH, FOX_WIDTH, FOX_WIDTH, FOX_HEADS, FOX_WIDTH,
          DSA_WIDTH, DSA_WIDTH, DSA_WIDTH, DSA_WIDTH,
          IDX_HEADS * IDX_DIM, IDX_DIM, IDX_HEADS)
V_PIECES = (2, 7)
SPLIT_POINTS = tuple(int(v) for v in np.cumsum(WIDTHS)[:-1])
IN_WIDTH = int(sum(WIDTHS))

N_PAGES = PAST_LEN // PAGE_SIZE
N_PHYS_PAGES = (DEC_BATCH * N_PAGES * 5) // 4

kernel_name = 'fox_dsa_parallel_heads_deepnorm_step'


def rope(x, pos):
    d = x.shape[-1]
    half = d // 2
    inv = ROPE_THETA ** (-jnp.arange(half, dtype=jnp.float32) * 2.0 / d)
    ang = pos.astype(jnp.float32)[:, None] * inv[None, :]
    cos = jnp.cos(ang)[:, None, :]
    sin = jnp.sin(ang)[:, None, :]
    xf = x.astype(jnp.float32)
    x1, x2 = xf[..., :half], xf[..., half:]
    return jnp.concatenate([x1 * cos - x2 * sin, x2 * cos + x1 * sin], axis=-1).astype(x.dtype)


def layer_norm(z, g, b):
    zf = z.astype(jnp.float32)
    mu = zf.mean(-1, keepdims=True)
    var = jnp.square(zf - mu).mean(-1, keepdims=True)
    return ((zf - mu) * lax.rsqrt(var + LN_EPS) * g.astype(jnp.float32) + b.astype(jnp.float32)).astype(z.dtype)


def project(x, pos, w_in, b_f):
    b, s = x.shape[0], x.shape[1]
    h = jnp.einsum('bsd,de->bse', x, w_in)
    fq, fk, fv, ff, fg, dq, dk, dv, dg, iq, ik, iw = jnp.split(h, SPLIT_POINTS, axis=-1)
    fq = fq.reshape(b, s, FOX_HEADS, HEAD_DIM)
    fk = fk.reshape(b, s, FOX_HEADS, HEAD_DIM)
    fv = fv.reshape(b, s, FOX_HEADS, HEAD_DIM)
    logf = jax.nn.log_sigmoid(ff.astype(jnp.float32) + b_f.astype(jnp.float32))
    dq = rope(dq.reshape(b, s, DSA_HEADS, HEAD_DIM), pos)
    dk = rope(dk.reshape(b, s, DSA_HEADS, HEAD_DIM), pos)
    dv = dv.reshape(b, s, DSA_HEADS, HEAD_DIM)
    iq = rope(iq.reshape(b, s, IDX_HEADS, IDX_DIM), pos)
    ik = rope(ik[:, :, None, :], pos)[:, :, 0, :]
    iw = iw.astype(jnp.float32) * IDX_W_SCALE
    return fq, fk, fv, logf, fg, dq, dk, dv, dg, iq, ik, iw


def output_and_norm(x, o_fox, fg, o_dsa, dg, w_out, ln_g, ln_b):
    b, s = x.shape[0], x.shape[1]
    mix = jnp.concatenate([o_fox.reshape(b, s, FOX_WIDTH) * jax.nn.silu(fg),
                           o_dsa.reshape(b, s, DSA_WIDTH) * jax.nn.silu(dg)], axis=-1)
    y = jnp.einsum('bse,ed->bsd', mix, w_out)
    return layer_norm(DEEPNORM_ALPHA * x + y, ln_g, ln_b)


def gather_rows(rows, idx):
    return jax.vmap(lambda r, i: r[i])(rows, idx)


def fox_prompt(q, k, v, logf):
    s = q.shape[1]
    c = jnp.cumsum(logf, axis=1).transpose(0, 2, 1)
    kpos = jnp.arange(s)

    def block(i):
        qs = i * Q_BLOCK
        qb = lax.dynamic_slice_in_dim(q, qs, Q_BLOCK, axis=1)
        cq = lax.dynamic_slice_in_dim(c, qs, Q_BLOCK, axis=2)
        qpos = qs + jnp.arange(Q_BLOCK)
        sc = (jnp.einsum('bqhd,bkhd->bhqk', qb, k).astype(jnp.float32) * ATTN_SCALE
              + (cq[..., :, None] - c[..., None, :]))
        sc = jnp.where(kpos[None, :] <= qpos[:, None], sc, -jnp.inf)
        p = jax.nn.softmax(sc, axis=-1).astype(v.dtype)
        return jnp.einsum('bhqk,bkhd->bqhd', p, v)

    o = lax.map(block, jnp.arange(s // Q_BLOCK))
    return jnp.moveaxis(o, 0, 1).reshape(q.shape)


def dsa_prompt(q, k, v, iq, ik, iw):
    s = q.shape[1]
    topk = min(TOPK_MAX, s // 4)
    kpos = jnp.arange(s)

    def block(i):
        qs = i * Q_BLOCK
        qb = lax.dynamic_slice_in_dim(q, qs, Q_BLOCK, axis=1)
        iqb = lax.dynamic_slice_in_dim(iq, qs, Q_BLOCK, axis=1)
        iwb = lax.dynamic_slice_in_dim(iw, qs, Q_BLOCK, axis=1)
        qpos = qs + jnp.arange(Q_BLOCK)
        rel = jax.nn.relu(jnp.einsum('bqhd,bkd->bqhk', iqb, ik).astype(jnp.float32))
        score = jnp.einsum('bqh,bqhk->bqk', iwb, rel)
        score = jnp.where(kpos[None, None, :] <= qpos[None, :, None], score, -jnp.inf)
        _, sel = lax.top_k(score, topk)
        valid = sel <= qpos[None, :, None]
        ks = gather_rows(k, sel)
        vs = gather_rows(v, sel)
        sc = jnp.einsum('bqhd,bqkhd->bhqk', qb, ks).astype(jnp.float32) * ATTN_SCALE
        sc = jnp.where(valid[:, None], sc, -jnp.inf)
        p = jax.nn.softmax(sc, axis=-1).astype(v.dtype)
        return jnp.einsum('bhqk,bqkhd->bqhd', p, vs)

    o = lax.map(block, jnp.arange(s // Q_BLOCK))
    return jnp.moveaxis(o, 0, 1).reshape(q.shape)


def softmax_partial(sc, vals):
    m = sc.max(-1)
    e = jnp.exp(sc - m[..., None])
    return m, e.sum(-1), jnp.einsum('bhqk,bkhd->bhqd', e, vals.astype(jnp.float32))


def fox_sample(q, k, v, logf, cache_k, cache_v, cache_logf, page_table, layer):
    b, t = q.shape[0], q.shape[1]
    nc = jnp.cumsum(logf, axis=1).transpose(0, 2, 1)
    pl = cache_logf[page_table, layer].astype(jnp.float32).reshape(b, PAST_LEN, FOX_HEADS)
    past_decay = jnp.flip(jnp.cumsum(jnp.flip(pl, 1), 1), 1) - pl
    pd = past_decay.reshape(b, N_PAGES, PAGE_SIZE, FOX_HEADS).transpose(1, 0, 3, 2)

    def page_part(args):
        phys, dec = args
        kp = cache_k[phys, layer].astype(q.dtype)
        vp = cache_v[phys, layer]
        sc = (jnp.einsum('bqhd,bkhd->bhqk', q, kp).astype(jnp.float32) * ATTN_SCALE
              + nc[..., :, None] + dec[:, :, None, :])
        return softmax_partial(sc, vp)

    m_p, l_p, a_p = lax.map(page_part, (page_table.T, pd))
    sc = (jnp.einsum('bqhd,bkhd->bhqk', q, k).astype(jnp.float32) * ATTN_SCALE
          + (nc[..., :, None] - nc[..., None, :]))
    causal = jnp.arange(t)[None, :] <= jnp.arange(t)[:, None]
    sc = jnp.where(causal, sc, -jnp.inf)
    m_n, l_n, a_n = softmax_partial(sc, v)
    m_all = jnp.concatenate([m_p, m_n[None]], axis=0)
    l_all = jnp.concatenate([l_p, l_n[None]], axis=0)
    a_all = jnp.concatenate([a_p, a_n[None]], axis=0)
    w = jnp.exp(m_all - m_all.max(0))
    o = (a_all * w[..., None]).sum(0) / (l_all * w).sum(0)[..., None]
    return o.transpose(0, 2, 1, 3).astype(q.dtype)


def dsa_sample(q, k, v, iq, ik, iw, cache_k, cache_v, cache_ik, page_table, layer):
    b, t = q.shape[0], q.shape[1]
    n_keys = PAST_LEN + t
    topk = min(TOPK_MAX, n_keys // 4)
    past_ik = cache_ik[page_table, layer].reshape(b, PAST_LEN, IDX_DIM)
    keys_ik = jnp.concatenate([past_ik.astype(ik.dtype), ik], axis=1)
    rel = jax.nn.relu(jnp.einsum('bqhd,bkd->bqhk', iq, keys_ik).astype(jnp.float32))
    score = jnp.einsum('bqh,bqhk->bqk', iw, rel)
    qpos = PAST_LEN + jnp.arange(t)
    kpos = jnp.arange(n_keys)
    score = jnp.where(kpos[None, None, :] <= qpos[None, :, None], score, -jnp.inf)
    _, sel = lax.top_k(score, topk)
    valid = sel <= qpos[None, :, None]
    in_past = sel < PAST_LEN
    sp = jnp.minimum(sel, PAST_LEN - 1)
    phys = jnp.take_along_axis(page_table, (sp // PAGE_SIZE).reshape(b, -1), axis=1).reshape(sel.shape)
    off = sp % PAGE_SIZE
    sn = jnp.clip(sel - PAST_LEN, 0, t - 1)
    ks = jnp.where(in_past[..., None, None], cache_k[phys, layer, off].astype(k.dtype), gather_rows(k, sn))
    vs = jnp.where(in_past[..., None, None], cache_v[phys, layer, off].astype(v.dtype), gather_rows(v, sn))
    sc = jnp.einsum('bqhd,bqkhd->bhqk', q, ks).astype(jnp.float32) * ATTN_SCALE
    sc = jnp.where(valid[:, None], sc, -jnp.inf)
    p = jax.nn.softmax(sc, axis=-1).astype(vs.dtype)
    return jnp.einsum('bhqk,bqkhd->bqhd', p, vs)


def setup_inputs(seed: int = 0) -> dict:
    key = jax.random.key(seed)
    ks = jax.random.split(key, 16 + len(WIDTHS))
    f32 = jnp.float32
    pool = (N_PHYS_PAGES, DEPTH, PAGE_SIZE)
    x_prompt = jax.random.normal(ks[0], (BATCH, SEQ, D_MODEL), f32)
    x_sample = jax.random.normal(ks[1], (DEC_BATCH, DEC_SEQ, D_MODEL), f32)
    cache_fox_k = jax.random.normal(ks[2], pool + (FOX_HEADS, HEAD_DIM), f32)
    cache_fox_v = DEEPNORM_BETA * jax.random.normal(ks[3], pool + (FOX_HEADS, HEAD_DIM), f32)
    cache_fox_logf = jax.nn.log_sigmoid(FORGET_BIAS_INIT + jax.random.normal(ks[4], pool + (FOX_HEADS,), f32))
    cache_dsa_k = jax.random.normal(ks[5], pool + (DSA_HEADS, HEAD_DIM), f32)
    cache_dsa_v = DEEPNORM_BETA * jax.random.normal(ks[6], pool + (DSA_HEADS, HEAD_DIM), f32)
    cache_idx_k = jax.random.normal(ks[7], pool + (IDX_DIM,), f32)
    page_table = jax.random.permutation(ks[8], N_PHYS_PAGES)[: DEC_BATCH * N_PAGES].reshape(DEC_BATCH, N_PAGES).astype(jnp.int32)
    pieces = []
    for i, w in enumerate(WIDTHS):
        scale = D_MODEL ** -0.5 * (DEEPNORM_BETA if i in V_PIECES else 1.0)
        pieces.append(scale * jax.random.normal(ks[16 + i], (DEPTH, D_MODEL, w), f32))
    w_in = jnp.concatenate(pieces, axis=-1)
    b_f = FORGET_BIAS_INIT + 0.1 * jax.random.normal(ks[9], (DEPTH, FOX_HEADS), f32)
    w_out = DEEPNORM_BETA * MIX_WIDTH ** -0.5 * jax.random.normal(ks[10], (DEPTH, MIX_WIDTH, D_MODEL), f32)
    ln_g = 1.0 + 0.02 * jax.random.normal(ks[11], (DEPTH, D_MODEL), f32)
    ln_b = 0.02 * jax.random.normal(ks[12], (DEPTH, D_MODEL), f32)
    return {'x_prompt': x_prompt, 'x_sample': x_sample,
            'cache_fox_k': cache_fox_k, 'cache_fox_v': cache_fox_v, 'cache_fox_logf': cache_fox_logf,
            'cache_dsa_k': cache_dsa_k, 'cache_dsa_v': cache_dsa_v, 'cache_idx_k': cache_idx_k,
            'page_table': page_table,
            'w_in': w_in, 'b_f': b_f, 'w_out': w_out, 'ln_g': ln_g, 'ln_b': ln_b}


def reference(x_prompt, x_sample, cache_fox_k, cache_fox_v, cache_fox_logf, cache_dsa_k, cache_dsa_v,
              cache_idx_k, page_table, w_in, b_f, w_out, ln_g, ln_b):
    pos_p = jnp.arange(x_prompt.shape[1])
    pos_s = PAST_LEN + jnp.arange(x_sample.shape[1])
    h_p, h_s = x_prompt, x_sample
    rows_p, rows_s = [], []
    for layer in range(DEPTH):
        fq, fk, fv, logf, fg, dq, dk, dv, dg, iq, ik, iw = project(h_p, pos_p, w_in[layer], b_f[layer])
        o_fox = fox_prompt(fq, fk, fv, logf)
        o_dsa = dsa_prompt(dq, dk, dv, iq, ik, iw)
        rows_p.append((fk, fv, logf, dk, dv, ik))
        h_p = output_and_norm(h_p, o_fox, fg, o_dsa, dg, w_out[layer], ln_g[layer], ln_b[layer])
        fq, fk, fv, logf, fg, dq, dk, dv, dg, iq, ik, iw = project(h_s, pos_s, w_in[layer], b_f[layer])
        o_fox = fox_sample(fq, fk, fv, logf, cache_fox_k, cache_fox_v, cache_fox_logf, page_table, layer)
        o_dsa = dsa_sample(dq, dk, dv, iq, ik, iw, cache_dsa_k, cache_dsa_v, cache_idx_k, page_table, layer)
        rows_s.append((fk, fv, logf, dk, dv, ik))
        h_s = output_and_norm(h_s, o_fox, fg, o_dsa, dg, w_out[layer], ln_g[layer], ln_b[layer])
    new_p = [jnp.stack(r, axis=1) for r in zip(*rows_p)]
    new_s = [jnp.stack(r, axis=1) for r in zip(*rows_s)]
    return (h_p, h_s, *new_p, *new_s)
```

```python
import functools
import math

import jax
import jax.numpy as jnp
import numpy as np
from jax import lax
from jax.experimental import pallas as pl
from jax.experimental.pallas import tpu as pltpu

F32 = jnp.float32
BF16 = jnp.bfloat16

D_MODEL = 2048
HEAD_DIM = 128
FOX_HEADS = 8
DSA_HEADS = 8
WIDTH = FOX_HEADS * HEAD_DIM
IDX_HEADS = 16
IDX_DIM = 64
TOPK_MAX = 256
PAGE_SIZE = 128
ROPE_THETA = 10000.0
LN_EPS = 1e-5
ATTN_SCALE = HEAD_DIM ** -0.5
IDX_W_SCALE = IDX_HEADS ** -0.5 * IDX_DIM ** -0.5
DEPTH = 1
DEEPNORM_ALPHA = (2.0 * DEPTH) ** 0.25
WIDTHS = (WIDTH, WIDTH, WIDTH, FOX_HEADS, WIDTH, WIDTH, WIDTH, WIDTH, WIDTH,
          IDX_HEADS * IDX_DIM, IDX_DIM, IDX_HEADS)
OFFS = tuple(int(v) for v in np.cumsum((0,) + WIDTHS))

LANES = 128
VMEM_LIMIT_BYTES = 56 * 1024 * 1024
MASK_NEG = -1e30
INT_MIN = -(2 ** 31)


def _cparams(sem):
    return pltpu.CompilerParams(dimension_semantics=sem, vmem_limit_bytes=VMEM_LIMIT_BYTES)


def _rope_tables(pos, dim):
    half = dim // 2
    inv = ROPE_THETA ** (-jnp.arange(half, dtype=F32) * 2.0 / dim)
    ang = pos.astype(F32)[:, None] * inv[None, :]
    cos, sin = jnp.cos(ang), jnp.sin(ang)
    zero = jnp.zeros_like(sin)
    if dim == LANES:
        return (jnp.concatenate([cos, cos], 1), jnp.concatenate([-sin, sin], 1))
    return (jnp.concatenate([cos, cos, cos, cos], 1),
            jnp.concatenate([-sin, zero, -sin, zero], 1),
            jnp.concatenate([zero, sin, zero, sin], 1))


def _rope128(blk, cos, sin):
    return blk * cos + pltpu.roll(blk, 64, axis=1) * sin


def _rope64(blk, cos, sin_lo, sin_hi):
    return blk * cos + pltpu.roll(blk, 96, axis=1) * sin_lo + pltpu.roll(blk, 32, axis=1) * sin_hi


def _proj_kernel(x_ref, w_ref, *refs, kind):
    xb = x_ref[...].astype(BF16)
    if kind == "plain":
        (o_ref,) = refs
        for c in range(0, WIDTH, 256):
            o_ref[:, c:c + 256] = jnp.dot(xb, w_ref[:, c:c + 256], preferred_element_type=F32)
    elif kind == "rope128":
        cos_ref, sin_ref, o_ref = refs
        cos, sin = cos_ref[...], sin_ref[...]
        for c in range(0, WIDTH, 256):
            acc = jnp.dot(xb, w_ref[:, c:c + 256], preferred_element_type=F32)
            for h in range(2):
                o_ref[:, c + h * 128:c + (h + 1) * 128] = _rope128(acc[:, h * 128:(h + 1) * 128], cos, sin)
    elif kind == "rope64":
        cos_ref, slo_ref, shi_ref, o_ref = refs
        cos, slo, shi = cos_ref[...], slo_ref[...], shi_ref[...]
        for c in range(0, WIDTH, 256):
            acc = jnp.dot(xb, w_ref[:, c:c + 256], preferred_element_type=F32)
            for h in range(2):
                o_ref[:, c + h * 128:c + (h + 1) * 128] = _rope64(acc[:, h * 128:(h + 1) * 128], cos, slo, shi)
    else:
        cos_ref, slo_ref, shi_ref, bf_ref, ik_ref, logf_ref, iw_ref = refs
        acc = jnp.dot(xb, w_ref[...], preferred_element_type=F32)
        ik = _rope64(acc[:, 0:128], cos_ref[...], slo_ref[...], shi_ref[...])
        ik_ref[...] = ik[:, :IDX_DIM]
        z = acc[:, 128:256] + bf_ref[...]
        logf = jnp.minimum(z, 0.0) - jnp.log(1.0 + jnp.exp(-jnp.abs(z)))
        logf_ref[...] = logf[:, :FOX_HEADS]
        iw_ref[...] = (acc[:, 256:384] * IDX_W_SCALE)[:, :IDX_HEADS]


def _project_segment(x, w, kind, tables, tm, extra=None):
    n = x.shape[0]
    cols = w.shape[1]
    grid = (n // tm,)
    tab_blocks = tables[0].shape[0] // tm if tables else 1
    row = lambda i: (i, 0)
    tab = lambda i: (i % tab_blocks, 0)
    in_specs = [pl.BlockSpec((tm, D_MODEL), row), pl.BlockSpec((D_MODEL, cols), lambda i: (0, 0))]
    in_specs += [pl.BlockSpec((tm, LANES), tab) for _ in tables]
    args = [x, w, *tables]
    if kind == "small":
        in_specs.append(pl.BlockSpec((1, LANES), lambda i: (0, 0)))
        args.append(extra)
        out_shape = (jax.ShapeDtypeStruct((n, IDX_DIM), F32), jax.ShapeDtypeStruct((n, FOX_HEADS), F32),
                     jax.ShapeDtypeStruct((n, IDX_HEADS), F32))
        out_specs = (pl.BlockSpec((tm, IDX_DIM), row), pl.BlockSpec((tm, FOX_HEADS), row),
                     pl.BlockSpec((tm, IDX_HEADS), row))
    else:
        out_shape = jax.ShapeDtypeStruct((n, cols), F32)
        out_specs = pl.BlockSpec((tm, cols), row)
    return pl.pallas_call(
        functools.partial(_proj_kernel, kind=kind),
        grid=grid, in_specs=in_specs, out_specs=out_specs, out_shape=out_shape,
        compiler_params=_cparams(("arbitrary",)), name=f"proj_{kind}",
    )(*args)


def _project(x, pos_rows, w_in, b_f, tm):
    seg = lambda i: w_in[:, OFFS[i]:OFFS[i + 1]].astype(BF16)
    t128 = _rope_tables(pos_rows, HEAD_DIM)
    t64 = _rope_tables(pos_rows, IDX_DIM)
    fq = _project_segment(x, seg(0), "plain", (), tm)
    fk = _project_segment(x, seg(1), "plain", (), tm)
    fv = _project_segment(x, seg(2), "plain", (), tm)
    fg = _project_segment(x, seg(4), "plain", (), tm)
    dq = _project_segment(x, seg(5), "rope128", t128, tm)
    dk = _project_segment(x, seg(6), "rope128", t128, tm)
    dv = _project_segment(x, seg(7), "plain", (), tm)
    dg = _project_segment(x, seg(8), "plain", (), tm)
    iq = _project_segment(x, seg(9), "rope64", t64, tm)
    pad = lambda a: jnp.pad(a, ((0, 0), (0, LANES - a.shape[1])))
    w_small = jnp.concatenate([pad(w_in[:, OFFS[10]:OFFS[11]]), pad(w_in[:, OFFS[3]:OFFS[4]]),
                               pad(w_in[:, OFFS[11]:OFFS[12]])], axis=1).astype(BF16)
    ik, logf, iw = _project_segment(x, w_small, "small", t64, tm, extra=pad(b_f[None, :].astype(F32)))
    return fq, fk, fv, logf, fg, dq, dk, dv, dg, iq, ik, iw


def _cumsum_kernel(x_ref, o_ref, *, suffix):
    rows, n = x_ref.shape
    r = lax.broadcasted_iota(jnp.int32, (LANES, LANES), 0)
    c = lax.broadcasted_iota(jnp.int32, (LANES, LANES), 1)
    tri = ((r > c) if suffix else (r <= c)).astype(F32)
    carry = jnp.zeros((rows, 1), F32)
    nchunks = n // LANES
    for step in range(nchunks):
        j = nchunks - 1 - step if suffix else step
        x = x_ref[:, j * LANES:(j + 1) * LANES]
        blk = jnp.dot(x, tri, precision=lax.Precision.HIGHEST, preferred_element_type=F32) + carry
        o_ref[:, j * LANES:(j + 1) * LANES] = blk
        carry = (blk[:, 0:1] + x[:, 0:1]) if suffix else blk[:, LANES - 1:LANES]


def _cumsum_lanes(x, suffix=False):
    rows, n = x.shape
    tr = min(rows, 64)
    return pl.pallas_call(
        functools.partial(_cumsum_kernel, suffix=suffix), grid=(rows // tr,),
        in_specs=[pl.BlockSpec((tr, n), lambda i: (i, 0))], out_specs=pl.BlockSpec((tr, n), lambda i: (i, 0)),
        out_shape=jax.ShapeDtypeStruct(x.shape, F32), compiler_params=_cparams(("arbitrary",)),
        name="cumsum_suffix" if suffix else "cumsum")(x)


def _flash_kernel(qi_tab, ki_tab, q_ref, k_ref, v_ref, g_ref, *refs, mode, T, nheads):
    if mode == "fox":
        ctok_ref, crow_ref, o_ref, m_scr, l_scr, acc_scr = refs
    else:
        mask_ref, o_ref, m_scr, l_scr, acc_scr = refs
    t = pl.program_id(1)
    qi, ki = qi_tab[t], ki_tab[t]

    @pl.when(ki == 0)
    def _():
        m_scr[...] = jnp.full(m_scr.shape, MASK_NEG, F32)
        l_scr[...] = jnp.zeros(l_scr.shape, F32)
        acc_scr[...] = jnp.zeros(acc_scr.shape, F32)

    def step(diag):
        if mode == "dsa":
            keep = mask_ref[0].astype(jnp.int32) != 0
        elif diag:
            keep = (lax.broadcasted_iota(jnp.int32, (T, T), 1) <= lax.broadcasted_iota(jnp.int32, (T, T), 0))
        for h in range(nheads):
            sl = slice(h * HEAD_DIM, (h + 1) * HEAD_DIM)
            qh = q_ref[0, :, sl].astype(BF16)
            kh = k_ref[0, :, sl].astype(BF16)
            vh = v_ref[0, :, sl].astype(BF16)
            s = lax.dot_general(qh, kh, (((1,), (1,)), ((), ())), preferred_element_type=F32) * ATTN_SCALE
            if mode == "fox":
                s = s + (ctok_ref[0, :, h:h + 1] - crow_ref[0, h:h + 1, :])
            if mode == "dsa" or diag:
                s = jnp.where(keep, s, MASK_NEG)
            m_prev = m_scr[h, :, 0:1]
            m_new = jnp.maximum(m_prev, jnp.max(s, axis=1, keepdims=True))
            p = jnp.exp(s - m_new)
            alpha = jnp.exp(m_prev - m_new)
            l_new = alpha * l_scr[h, :, 0:1] + jnp.sum(p, axis=1, keepdims=True)
            acc_scr[:, sl] = alpha * acc_scr[:, sl] + jnp.dot(p.astype(BF16), vh, preferred_element_type=F32)
            m_scr[h] = jnp.broadcast_to(m_new, (T, LANES))
            l_scr[h] = jnp.broadcast_to(l_new, (T, LANES))

    if mode == "dsa":
        step(False)
    else:
        pl.when(ki == qi)(functools.partial(step, True))
        pl.when(ki != qi)(functools.partial(step, False))

    @pl.when(ki == qi)
    def _():
        for h in range(nheads):
            sl = slice(h * HEAD_DIM, (h + 1) * HEAD_DIM)
            g = g_ref[0, :, sl]
            o = acc_scr[:, sl] / l_scr[h, :, 0:1]
            o_ref[0, :, sl] = (o * (g / (1.0 + jnp.exp(-g)))).astype(BF16)


def _flash(mode, q, k, v, gate, extra, T):
    b, s, w = q.shape
    nheads = w // HEAD_DIM
    nb = s // T
    pairs = [(i, j) for i in range(nb) for j in range(i + 1)]
    qi_tab = jnp.asarray([p[0] for p in pairs], jnp.int32)
    ki_tab = jnp.asarray([p[1] for p in pairs], jnp.int32)
    qmap = lambda bb, t, qt, kt: (bb, qt[t], 0)
    kmap = lambda bb, t, qt, kt: (bb, kt[t], 0)
    in_specs = [pl.BlockSpec((1, T, w), qmap), pl.BlockSpec((1, T, w), kmap), pl.BlockSpec((1, T, w), kmap),
                pl.BlockSpec((1, T, w), qmap)]
    if mode == "fox":
        c_tok, c_row = extra
        in_specs += [pl.BlockSpec((1, T, nheads), qmap),
                     pl.BlockSpec((1, nheads, T), lambda bb, t, qt, kt: (bb, 0, kt[t]))]
        args = (c_tok, c_row)
    else:
        in_specs += [pl.BlockSpec((1, T, T), lambda bb, t, qt, kt: (bb, qt[t], kt[t]))]
        args = (extra,)
    grid_spec = pltpu.PrefetchScalarGridSpec(
        num_scalar_prefetch=2, grid=(b, len(pairs)), in_specs=in_specs,
        out_specs=pl.BlockSpec((1, T, w), qmap),
        scratch_shapes=[pltpu.VMEM((nheads, T, LANES), F32), pltpu.VMEM((nheads, T, LANES), F32),
                        pltpu.VMEM((T, w), F32)])
    return pl.pallas_call(
        functools.partial(_flash_kernel, mode=mode, T=T, nheads=nheads),
        grid_spec=grid_spec, out_shape=jax.ShapeDtypeStruct((b, s, w), BF16),
        compiler_params=_cparams(("arbitrary", "arbitrary")), name=f"flash_{mode}",
    )(qi_tab, ki_tab, q, k, v, gate, *args)


def _sortable_key(x):
    bits = pltpu.bitcast(x, jnp.int32)
    return bits ^ ((bits >> 31) & 0x7FFFFFFF)


def _kth_threshold(count_ge, topk, rows):
    t0 = jnp.where(count_ge(jnp.zeros((rows, 1), jnp.int32)) >= topk, 0, INT_MIN).astype(jnp.int32)

    def body(i, t):
        cand = t | (jnp.int32(1) << (30 - i))
        return jnp.where(count_ge(cand) >= topk, cand, t)

    return lax.fori_loop(0, 31, body, t0)


def _topk_cut(count, topk, rows, ncols, jstar_scr):
    thr = _kth_threshold(lambda cand: count(lambda key, col: key >= cand), topk, rows)
    n_gt = count(lambda key, col: key > thr)
    n_ge = count(lambda key, col: key >= thr)
    real = thr > INT_MIN
    tie = jnp.logical_and(real, n_ge > topk)
    need = topk - n_gt
    jstar_scr[...] = jnp.broadcast_to(jnp.where(real, ncols - 1, -1).astype(jnp.int32), jstar_scr.shape)

    @pl.when(jnp.max(tie.astype(jnp.int32)) > 0)
    def _():
        nbits = max(1, (ncols - 1).bit_length())

        def body(b, j):
            cand = j | (jnp.int32(1) << (nbits - 1 - b))
            below = count(lambda key, col: jnp.logical_and(key == thr, col < cand))
            return jnp.where(below < need, cand, j)

        j = lax.fori_loop(0, nbits, body, jnp.zeros((rows, 1), jnp.int32))
        jstar_scr[...] = jnp.broadcast_to(jnp.where(tie, j, jstar_scr[:, 0:1]), jstar_scr.shape)

    return thr, jstar_scr[:, 0:1]


def _selected(key, col, thr, jstar):
    return jnp.where(key == thr, (col <= jstar).astype(jnp.int32), (key > thr).astype(jnp.int32))


def _select_kernel(iq_ref, iw_ref, ik2_ref, mask_ref, keys_scr, jstar_scr, *, tq, tk, topk, seq):
    i = pl.program_id(1)
    nch = ((i + 1) * tq + tk - 1) // tk
    iqb = iq_ref[0].astype(BF16)
    w = iw_ref[0]
    row = i * tq + lax.broadcasted_iota(jnp.int32, (tq, 1), 0)
    lane = lax.broadcasted_iota(jnp.int32, (1, tk), 1)

    def score_chunk(c, carry):
        off = pl.multiple_of(c * tk, tk)
        k2 = ik2_ref[0, :, pl.ds(pl.multiple_of(c * 2 * tk, 2 * tk), 2 * tk)]
        acc = jnp.zeros((tq, tk), F32)
        for pr in range(IDX_HEADS // 2):
            r = jnp.dot(iqb[:, pr * 128:(pr + 1) * 128], k2, preferred_element_type=F32)
            acc = acc + w[:, 2 * pr:2 * pr + 1] * jnp.maximum(r[:, :tk], 0.0)
            acc = acc + w[:, 2 * pr + 1:2 * pr + 2] * jnp.maximum(r[:, tk:], 0.0)
        key = jnp.where(off + lane <= row, _sortable_key(acc), INT_MIN)
        keys_scr[:, pl.ds(off, tk)] = key
        return carry

    lax.fori_loop(0, nch, score_chunk, 0)

    def count(pred):
        def body(c, acc):
            off = pl.multiple_of(c * tk, tk)
            hit = pred(keys_scr[:, pl.ds(off, tk)], off + lane).astype(jnp.int32)
            part = hit[:, 0:LANES]
            for j in range(1, tk // LANES):
                part = part + hit[:, j * LANES:(j + 1) * LANES]
            return acc + part
        acc = lax.fori_loop(0, nch, body, jnp.zeros((tq, LANES), jnp.int32))
        return jnp.sum(acc, axis=1, keepdims=True)

    thr, jstar = _topk_cut(count, topk, tq, seq, jstar_scr)

    def emit(c, carry):
        off = pl.multiple_of(c * tk, tk)
        sel = _selected(keys_scr[:, pl.ds(off, tk)], off + lane, thr, jstar)
        mask_ref[0, :, pl.ds(off, tk)] = sel.astype(jnp.int8)
        return carry

    lax.fori_loop(0, nch, emit, 0)

    def clear(c, carry):
        mask_ref[0, :, pl.ds(pl.multiple_of(c * tk, tk), tk)] = jnp.zeros((tq, tk), jnp.int8)
        return carry

    lax.fori_loop(nch, seq // tk, clear, 0)


def _block_diag_keys(ik, tk):
    b, s, d = ik.shape
    kt = jnp.swapaxes(ik.astype(BF16), 1, 2).reshape(b, d, s // tk, tk)
    z = jnp.zeros_like(kt)
    top = jnp.stack([kt, z], axis=3)
    bot = jnp.stack([z, kt], axis=3)
    return jnp.concatenate([top, bot], axis=1).reshape(b, 2 * d, 2 * s)


def _select_prompt(iq, iw, ik, topk, tq, tk):
    b, s, _ = iq.shape
    ik2 = _block_diag_keys(ik, tk)
    return pl.pallas_call(
        functools.partial(_select_kernel, tq=tq, tk=tk, topk=topk, seq=s),
        grid=(b, s // tq),
        in_specs=[pl.BlockSpec((1, tq, IDX_HEADS * IDX_DIM), lambda bb, i: (bb, i, 0)),
                  pl.BlockSpec((1, tq, IDX_HEADS), lambda bb, i: (bb, i, 0)),
                  pl.BlockSpec((1, 2 * IDX_DIM, 2 * s), lambda bb, i: (bb, 0, 0))],
        out_specs=pl.BlockSpec((1, tq, s), lambda bb, i: (bb, i, 0)),
        out_shape=jax.ShapeDtypeStruct((b, s, s), jnp.int8),
        scratch_shapes=[pltpu.VMEM((tq, s), jnp.int32), pltpu.VMEM((tq, LANES), jnp.int32)],
        compiler_params=_cparams(("arbitrary", "arbitrary")), name="select_prompt",
    )(iq, iw, ik2)


def _outnorm_kernel(x_ref, mf_ref, md_ref, w1_ref, w2_ref, g_ref, b_ref, o_ref):
    y = jnp.dot(mf_ref[...].astype(BF16), w1_ref[...], preferred_element_type=F32)
    y = y + jnp.dot(md_ref[...].astype(BF16), w2_ref[...], preferred_element_type=F32)
    z = DEEPNORM_ALPHA * x_ref[...] + y
    mu = jnp.mean(z, axis=1, keepdims=True)
    zc = z - mu
    var = jnp.mean(zc * zc, axis=1, keepdims=True)
    o_ref[...] = zc * lax.rsqrt(var + LN_EPS) * g_ref[...] + b_ref[...]


def _output_and_norm(x, mix_fox, mix_dsa, w_out, ln_g, ln_b, tm):
    n = x.shape[0]
    w1 = w_out[:WIDTH].astype(BF16)
    w2 = w_out[WIDTH:].astype(BF16)
    row = lambda i: (i, 0)
    const = lambda i: (0, 0)
    return pl.pallas_call(
        _outnorm_kernel, grid=(n // tm,),
        in_specs=[pl.BlockSpec((tm, D_MODEL), row), pl.BlockSpec((tm, WIDTH), row), pl.BlockSpec((tm, WIDTH), row),
                  pl.BlockSpec((WIDTH, D_MODEL), const), pl.BlockSpec((WIDTH, D_MODEL), const),
                  pl.BlockSpec((1, D_MODEL), const), pl.BlockSpec((1, D_MODEL), const)],
        out_specs=pl.BlockSpec((tm, D_MODEL), row),
        out_shape=jax.ShapeDtypeStruct((n, D_MODEL), F32),
        compiler_params=_cparams(("arbitrary",)), name="outnorm",
    )(x, mix_fox, mix_dsa, w1, w2, ln_g[None, :].astype(F32), ln_b[None, :].astype(F32))


def _prompt_tiles(s):
    tm = min(512, s)
    t_attn = min(512, s)
    tq_sel = min(256, s)
    tk_sel = min(512, s)
    return tm, t_attn, tq_sel, tk_sel


def _prompt_group(x, w_in, b_f, w_out, ln_g, ln_b):
    b, s, d = x.shape
    tm, t_attn, tq_sel, tk_sel = _prompt_tiles(s)
    x2 = x.reshape(b * s, d)
    fq, fk, fv, logf, fg, dq, dk, dv, dg, iq, ik, iw = _project(x2, jnp.arange(s), w_in, b_f, tm)
    r3 = lambda a: a.reshape(b, s, a.shape[-1])
    c_row = _cumsum_lanes(jnp.swapaxes(r3(logf), 1, 2).reshape(b * FOX_HEADS, s)).reshape(b, FOX_HEADS, s)
    c_tok = jnp.swapaxes(c_row, 1, 2)
    mix_fox = _flash("fox", r3(fq), r3(fk), r3(fv), r3(fg), (c_tok, c_row), t_attn)
    topk = min(TOPK_MAX, s // 4)
    mask = _select_prompt(r3(iq), r3(iw), r3(ik), topk, tq_sel, tk_sel)
    mix_dsa = _flash("dsa", r3(dq), r3(dk), r3(dv), r3(dg), mask, t_attn)
    y = _output_and_norm(x2, mix_fox.reshape(b * s, WIDTH), mix_dsa.reshape(b * s, WIDTH), w_out, ln_g, ln_b, tm)
    heads = lambda a, h: a.reshape(b, 1, s, h, HEAD_DIM)
    return (y.reshape(b, s, d), heads(fk, FOX_HEADS), heads(fv, FOX_HEADS), logf.reshape(b, 1, s, FOX_HEADS),
            heads(dk, DSA_HEADS), heads(dv, DSA_HEADS), ik.reshape(b, 1, s, IDX_DIM))


def _page_specs(block, npages_per_step, n_pages):
    zeros = (0,) * (len(block) - 1)
    return [pl.BlockSpec(block, lambda b, s, pt, g=g: (pt[b * n_pages + s * npages_per_step + g],) + zeros)
            for g in range(npages_per_step)]


def _gather_logf_kernel(pt_ref, *refs):
    pages, o_ref = refs[:-1], refs[-1]
    eye = (lax.broadcasted_iota(jnp.int32, (FOX_HEADS, FOX_HEADS), 0)
           == lax.broadcasted_iota(jnp.int32, (FOX_HEADS, FOX_HEADS), 1)).astype(F32)
    for g, page in enumerate(pages):
        o_ref[0, :, g * PAGE_SIZE:(g + 1) * PAGE_SIZE] = lax.dot_general(
            eye, page[0], (((1,), (1,)), ((), ())), precision=lax.Precision.HIGHEST, preferred_element_type=F32)


def _gather_logf(cache_logf, pt_flat, db, n_pages, gp):
    grid_spec = pltpu.PrefetchScalarGridSpec(
        num_scalar_prefetch=1, grid=(db, n_pages // gp),
        in_specs=_page_specs((1, PAGE_SIZE, FOX_HEADS), gp, n_pages),
        out_specs=pl.BlockSpec((1, FOX_HEADS, gp * PAGE_SIZE), lambda b, s, pt: (b, 0, s)))
    return pl.pallas_call(
        _gather_logf_kernel, grid_spec=grid_spec,
        out_shape=jax.ShapeDtypeStruct((db, FOX_HEADS, n_pages * PAGE_SIZE), F32),
        compiler_params=_cparams(("arbitrary", "arbitrary")), name="gather_logf",
    )(pt_flat, *([cache_logf] * gp))


def _select_sample_kernel(pt_ref, iqm_ref, wcol_ref, iknew_ref, *refs, gs, topk, past, T):
    pages = refs[:gs]
    mp_ref, mn_ref, keys_scr, jstar_scr = refs[gs:]
    s_idx = pl.program_id(1)
    iqm = iqm_ref[0]
    wcol = wcol_ref[0]
    ncols = past + PAGE_SIZE

    def scores(kblk):
        r = lax.dot_general(iqm, kblk.astype(BF16), (((1,), (1,)), ((), ())), preferred_element_type=F32)
        r = wcol * jnp.maximum(r, 0.0)
        acc = r[0:T]
        for h in range(1, IDX_HEADS):
            acc = acc + r[h * T:(h + 1) * T]
        return acc

    for g in range(gs):
        off = pl.multiple_of((s_idx * gs + g) * PAGE_SIZE, PAGE_SIZE)
        keys_scr[:, pl.ds(off, PAGE_SIZE)] = _sortable_key(scores(pages[g][0]))

    @pl.when(s_idx == pl.num_programs(1) - 1)
    def _():
        t_col = lax.broadcasted_iota(jnp.int32, (T, 1), 0)
        j_row = lax.broadcasted_iota(jnp.int32, (1, PAGE_SIZE), 1)
        keys_scr[:, past:ncols] = jnp.where(j_row <= t_col, _sortable_key(scores(iknew_ref[0])), INT_MIN)
        col = lax.broadcasted_iota(jnp.int32, (1, ncols), 1)

        def count(pred):
            return jnp.sum(pred(keys_scr[...], col).astype(jnp.int32), axis=1, keepdims=True)

        thr, jstar = _topk_cut(count, topk, T, ncols, jstar_scr)
        sel = _selected(keys_scr[...], col, thr, jstar).astype(F32)
        mp_ref[0] = sel[:, :past]
        mn_ref[0] = sel[:, past:]


def _select_sample(iq, iw, ik_new, cache_ik, pt_flat, n_pages, topk, gs):
    db, T, _ = iq.shape
    past = n_pages * PAGE_SIZE
    rows = IDX_HEADS * T
    iqm = jnp.swapaxes(iq.reshape(db, T, IDX_HEADS, IDX_DIM), 1, 2).reshape(db, rows, IDX_DIM).astype(BF16)
    wcol = jnp.swapaxes(iw, 1, 2).reshape(db, rows, 1)
    ik_pad = jnp.pad(ik_new, ((0, 0), (0, PAGE_SIZE - T), (0, 0)))
    per_b = lambda blk: pl.BlockSpec(blk, lambda b, s, pt: (b, 0, 0))
    grid_spec = pltpu.PrefetchScalarGridSpec(
        num_scalar_prefetch=1, grid=(db, n_pages // gs),
        in_specs=[per_b((1, rows, IDX_DIM)), per_b((1, rows, 1)), per_b((1, PAGE_SIZE, IDX_DIM))]
        + _page_specs((1, PAGE_SIZE, IDX_DIM), gs, n_pages),
        out_specs=(per_b((1, T, past)), per_b((1, T, PAGE_SIZE))),
        scratch_shapes=[pltpu.VMEM((T, past + PAGE_SIZE), jnp.int32), pltpu.VMEM((T, LANES), jnp.int32)])
    return pl.pallas_call(
        functools.partial(_select_sample_kernel, gs=gs, topk=topk, past=past, T=T), grid_spec=grid_spec,
        out_shape=(jax.ShapeDtypeStruct((db, T, past), F32), jax.ShapeDtypeStruct((db, T, PAGE_SIZE), F32)),
        compiler_params=_cparams(("arbitrary", "arbitrary")), name="select_sample",
    )(pt_flat, iqm, wcol, ik_pad, *([cache_ik] * gs))


def _decode_kernel(pt_ref, qbd_ref, knew_ref, vnew_ref, gate_ref, *refs, mode, G, nheads, T):
    if mode == "fox":
        pd_ref, lf_ref, lft_ref = refs[:3]
        refs = refs[3:]
    else:
        mp_ref, mn_ref = refs[:2]
        refs = refs[2:]
    k_refs, v_refs = refs[:G], refs[G:2 * G]
    o_ref, m_scr, l_scr, acc_scr = refs[2 * G:]
    s_idx = pl.program_id(1)
    rows = nheads * T
    nt = (((1,), (1,)), ((), ()))

    @pl.when(s_idx == 0)
    def _():
        m_scr[...] = jnp.full(m_scr.shape, MASK_NEG, F32)
        l_scr[...] = jnp.zeros(l_scr.shape, F32)
        acc_scr[...] = jnp.zeros(acc_scr.shape, F32)

    qbd = qbd_ref[0]
    if mode == "fox":
        ti = lax.broadcasted_iota(jnp.int32, (T, T), 0)
        tj = lax.broadcasted_iota(jnp.int32, (T, T), 1)
        nc = jnp.dot((tj <= ti).astype(F32), lf_ref[0], precision=lax.Precision.HIGHEST, preferred_element_type=F32)
        ui = lax.broadcasted_iota(jnp.int32, (T, PAGE_SIZE), 0)
        uj = lax.broadcasted_iota(jnp.int32, (T, PAGE_SIZE), 1)
        nct = jnp.dot(lft_ref[0], jnp.logical_and(ui <= uj, uj < T).astype(F32),
                      precision=lax.Precision.HIGHEST, preferred_element_type=F32)

    def update(s, vals):
        m_prev = m_scr[:, 0:1]
        m_new = jnp.maximum(m_prev, jnp.max(s, axis=1, keepdims=True))
        p = jnp.exp(s - m_new)
        alpha = jnp.exp(m_prev - m_new)
        l_scr[...] = jnp.broadcast_to(alpha * l_scr[:, 0:1] + jnp.sum(p, axis=1, keepdims=True), l_scr.shape)
        p = p.astype(BF16)
        pv = jnp.dot(p[:, 0:PAGE_SIZE], vals[0], preferred_element_type=F32)
        for g in range(1, len(vals)):
            pv = pv + jnp.dot(p[:, g * PAGE_SIZE:(g + 1) * PAGE_SIZE], vals[g], preferred_element_type=F32)
        acc_scr[...] = alpha * acc_scr[...] + pv
        m_scr[...] = jnp.broadcast_to(m_new, m_scr.shape)

    s = jnp.concatenate([lax.dot_general(qbd, k[0].astype(BF16), nt, preferred_element_type=F32) for k in k_refs],
                        axis=1) * ATTN_SCALE
    slabs = []
    for h in range(nheads):
        sh = s[h * T:(h + 1) * T]
        if mode == "fox":
            sh = sh + nc[:, h:h + 1] + pd_ref[0, h:h + 1, :]
        else:
            sh = jnp.where(mp_ref[0] > 0.5, sh, MASK_NEG)
        slabs.append(sh)
    update(jnp.concatenate(slabs, axis=0), [v[0].astype(BF16) for v in v_refs])

    @pl.when(s_idx == pl.num_programs(1) - 1)
    def _():
        sn = lax.dot_general(qbd, knew_ref[0].astype(BF16), nt, preferred_element_type=F32) * ATTN_SCALE
        t_col = lax.broadcasted_iota(jnp.int32, (T, 1), 0)
        j_row = lax.broadcasted_iota(jnp.int32, (1, PAGE_SIZE), 1)
        causal = j_row <= t_col
        slabs = []
        for h in range(nheads):
            sh = sn[h * T:(h + 1) * T]
            if mode == "fox":
                sh = jnp.where(causal, sh + (nc[:, h:h + 1] - nct[h:h + 1, :]), MASK_NEG)
            else:
                sh = jnp.where(mn_ref[0] > 0.5, sh, MASK_NEG)
            slabs.append(sh)
        update(jnp.concatenate(slabs, axis=0), [vnew_ref[0].astype(BF16)])
        for h in range(nheads):
            sl = slice(h * HEAD_DIM, (h + 1) * HEAD_DIM)
            o = acc_scr[h * T:(h + 1) * T, sl] / l_scr[h * T:(h + 1) * T, 0:1]
            g = gate_ref[0, :, sl]
            o_ref[0, :, sl] = o * (g / (1.0 + jnp.exp(-g)))


def _decode(mode, q, k_new, v_new, gate, extra, cache_k, cache_v, pt_flat, n_pages, G):
    db, T, w = q.shape
    nheads = w // HEAD_DIM
    rows = nheads * T
    eye = jnp.eye(nheads, dtype=q.dtype)
    qbd = jnp.einsum("bthd,hg->bhtgd", q.reshape(db, T, nheads, HEAD_DIM), eye).reshape(db, rows, w).astype(BF16)
    pad = lambda a: jnp.pad(a, ((0, 0), (0, PAGE_SIZE - T), (0, 0)))
    per_b = lambda blk: pl.BlockSpec(blk, lambda b, s, pt: (b, 0, 0))
    step = lambda blk: pl.BlockSpec(blk, lambda b, s, pt: (b, 0, s))
    in_specs = [per_b((1, rows, w)), per_b((1, PAGE_SIZE, w)), per_b((1, PAGE_SIZE, w)), per_b((1, T, w))]
    if mode == "fox":
        pd, logf_new = extra
        in_specs += [step((1, nheads, G * PAGE_SIZE)), per_b((1, T, nheads)), per_b((1, nheads, T))]
        args = (pd, logf_new, jnp.swapaxes(logf_new, 1, 2))
    else:
        mask_past, mask_new = extra
        in_specs += [step((1, T, G * PAGE_SIZE)), per_b((1, T, PAGE_SIZE))]
        args = (mask_past, mask_new)
    in_specs += _page_specs((1, PAGE_SIZE, w), G, n_pages) + _page_specs((1, PAGE_SIZE, w), G, n_pages)
    grid_spec = pltpu.PrefetchScalarGridSpec(
        num_scalar_prefetch=1, grid=(db, n_pages // G), in_specs=in_specs, out_specs=per_b((1, T, w)),
        scratch_shapes=[pltpu.VMEM((rows, LANES), F32), pltpu.VMEM((rows, LANES), F32), pltpu.VMEM((rows, w), F32)])
    return pl.pallas_call(
        functools.partial(_decode_kernel, mode=mode, G=G, nheads=nheads, T=T), grid_spec=grid_spec,
        out_shape=jax.ShapeDtypeStruct((db, T, w), F32),
        compiler_params=_cparams(("arbitrary", "arbitrary")), name=f"decode_{mode}",
    )(pt_flat, qbd, pad(k_new), pad(v_new), gate, *args, *([cache_k] * G), *([cache_v] * G))


def _sample_group(x, caches, page_table, w_in, b_f, w_out, ln_g, ln_b):
    cache_fox_k, cache_fox_v, cache_fox_logf, cache_dsa_k, cache_dsa_v, cache_idx_k = caches
    db, T, d = x.shape
    n_pages = page_table.shape[1]
    past = n_pages * PAGE_SIZE
    nphys = cache_fox_k.shape[0]
    G = min(8, n_pages)
    pt_flat = page_table.reshape(-1).astype(jnp.int32)
    x2 = x.reshape(db * T, d)
    pos = jnp.tile(past + jnp.arange(T), db)
    fq, fk, fv, logf, fg, dq, dk, dv, dg, iq, ik, iw = _project(x2, pos, w_in, b_f, db * T)
    r3 = lambda a: a.reshape(db, T, a.shape[-1])
    flat = lambda c: c.reshape(nphys, PAGE_SIZE, -1)
    past_logf = _gather_logf(flat(cache_fox_logf), pt_flat, db, n_pages, min(16, n_pages))
    pd = _cumsum_lanes(past_logf.reshape(db * FOX_HEADS, past), suffix=True).reshape(db, FOX_HEADS, past)
    mix_fox = _decode("fox", r3(fq), r3(fk), r3(fv), r3(fg), (pd, r3(logf)),
                      flat(cache_fox_k), flat(cache_fox_v), pt_flat, n_pages, G)
    topk = min(TOPK_MAX, (past + T) // 4)
    masks = _select_sample(r3(iq), r3(iw), r3(ik), flat(cache_idx_k), pt_flat, n_pages, topk, min(16, n_pages))
    mix_dsa = _decode("dsa", r3(dq), r3(dk), r3(dv), r3(dg), masks,
                      flat(cache_dsa_k), flat(cache_dsa_v), pt_flat, n_pages, G)
    y = _output_and_norm(x2, mix_fox.reshape(db * T, WIDTH), mix_dsa.reshape(db * T, WIDTH), w_out, ln_g, ln_b,
                         db * T)
    heads = lambda a, h: a.reshape(db, 1, T, h, HEAD_DIM)
    return (y.reshape(db, T, d), heads(fk, FOX_HEADS), heads(fv, FOX_HEADS), logf.reshape(db, 1, T, FOX_HEADS),
            heads(dk, DSA_HEADS), heads(dv, DSA_HEADS), ik.reshape(db, 1, T, IDX_DIM))


def kernel(x_prompt, x_sample, cache_fox_k, cache_fox_v, cache_fox_logf, cache_dsa_k, cache_dsa_v, cache_idx_k,
           page_table, w_in, b_f, w_out, ln_g, ln_b):
    assert w_in.shape[0] == DEPTH and cache_fox_k.shape[1] == DEPTH
    params = (w_in[0], b_f[0], w_out[0], ln_g[0], ln_b[0])
    caches = (cache_fox_k, cache_fox_v, cache_fox_logf, cache_dsa_k, cache_dsa_v, cache_idx_k)
    p = _prompt_group(x_prompt, *params)
    s = _sample_group(x_sample, caches, page_table, *params)
    return (p[0], s[0], *p[1:], *s[1:])
```

```python
import functools
import math

import jax
import jax.numpy as jnp
import numpy as np
from jax import lax
from jax.experimental import pallas as pl
from jax.experimental.pallas import tpu as pltpu

F32 = jnp.float32
BF16 = jnp.bfloat16

D_MODEL = 2048
HEAD_DIM = 128
FOX_HEADS = 8
DSA_HEADS = 8
WIDTH = FOX_HEADS * HEAD_DIM
IDX_HEADS = 16
IDX_DIM = 64
TOPK_MAX = 256
PAGE_SIZE = 128
ROPE_THETA = 10000.0
LN_EPS = 1e-5
ATTN_SCALE = HEAD_DIM ** -0.5
LOG2E = math.log2(math.e)
IDX_W_SCALE = IDX_HEADS ** -0.5 * IDX_DIM ** -0.5
DEPTH = 1
DEEPNORM_ALPHA = (2.0 * DEPTH) ** 0.25
WIDTHS = (WIDTH, WIDTH, WIDTH, FOX_HEADS, WIDTH, WIDTH, WIDTH, WIDTH, WIDTH,
          IDX_HEADS * IDX_DIM, IDX_DIM, IDX_HEADS)
OFFS = tuple(int(v) for v in np.cumsum((0,) + WIDTHS))

LANES = 128
VMEM_LIMIT_BYTES = 56 * 1024 * 1024
MASK_NEG = -1e30
INT_MIN = -(2 ** 31)


def _cparams(sem):
    return pltpu.CompilerParams(dimension_semantics=sem, vmem_limit_bytes=VMEM_LIMIT_BYTES)


def _rope_tables(pos, dim):
    half = dim // 2
    inv = ROPE_THETA ** (-jnp.arange(half, dtype=F32) * 2.0 / dim)
    ang = pos.astype(F32)[:, None] * inv[None, :]
    cos, sin = jnp.cos(ang), jnp.sin(ang)
    zero = jnp.zeros_like(sin)
    if dim == LANES:
        return (jnp.concatenate([cos, cos], 1), jnp.concatenate([-sin, sin], 1))
    return (jnp.concatenate([cos, cos, cos, cos], 1),
            jnp.concatenate([-sin, zero, -sin, zero], 1),
            jnp.concatenate([zero, sin, zero, sin], 1))


def _rope128(blk, cos, sin):
    return blk * cos + pltpu.roll(blk, 64, axis=1) * sin


def _rope64(blk, cos, sin_lo, sin_hi):
    return blk * cos + pltpu.roll(blk, 96, axis=1) * sin_lo + pltpu.roll(blk, 32, axis=1) * sin_hi


def _proj_kernel(x_ref, w_ref, *refs, kind):
    xb = x_ref[...].astype(BF16)
    if kind == "plain":
        (o_ref,) = refs
        for c in range(0, WIDTH, 256):
            o_ref[:, c:c + 256] = jnp.dot(xb, w_ref[:, c:c + 256], preferred_element_type=F32)
    elif kind == "rope128":
        cos_ref, sin_ref, o_ref = refs
        cos, sin = cos_ref[...], sin_ref[...]
        for c in range(0, WIDTH, 256):
            acc = jnp.dot(xb, w_ref[:, c:c + 256], preferred_element_type=F32)
            for h in range(2):
                o_ref[:, c + h * 128:c + (h + 1) * 128] = _rope128(acc[:, h * 128:(h + 1) * 128], cos, sin)
    elif kind == "rope64":
        cos_ref, slo_ref, shi_ref, o_ref = refs
        cos, slo, shi = cos_ref[...], slo_ref[...], shi_ref[...]
        for c in range(0, WIDTH, 256):
            acc = jnp.dot(xb, w_ref[:, c:c + 256], preferred_element_type=F32)
            for h in range(2):
                o_ref[:, c + h * 128:c + (h + 1) * 128] = _rope64(acc[:, h * 128:(h + 1) * 128], cos, slo, shi)
    else:
        cos_ref, slo_ref, shi_ref, bf_ref, ik_ref, logf_ref, iw_ref = refs
        acc = jnp.dot(xb, w_ref[...], preferred_element_type=F32)
        ik = _rope64(acc[:, 0:128], cos_ref[...], slo_ref[...], shi_ref[...])
        ik_ref[...] = ik[:, :IDX_DIM]
        z = acc[:, 128:256] + bf_ref[...]
        logf = jnp.minimum(z, 0.0) - jnp.log(1.0 + jnp.exp(-jnp.abs(z)))
        logf_ref[...] = logf[:, :FOX_HEADS]
        iw_ref[...] = (acc[:, 256:384] * IDX_W_SCALE)[:, :IDX_HEADS]


def _project_segment(x, w, kind, tables, tm, extra=None):
    n = x.shape[0]
    cols = w.shape[1]
    grid = (n // tm,)
    tab_blocks = tables[0].shape[0] // tm if tables else 1
    row = lambda i: (i, 0)
    tab = lambda i: (i % tab_blocks, 0)
    in_specs = [pl.BlockSpec((tm, D_MODEL), row), pl.BlockSpec((D_MODEL, cols), lambda i: (0, 0))]
    in_specs += [pl.BlockSpec((tm, LANES), tab) for _ in tables]
    args = [x, w, *tables]
    if kind == "small":
        in_specs.append(pl.BlockSpec((1, LANES), lambda i: (0, 0)))
        args.append(extra)
        out_shape = (jax.ShapeDtypeStruct((n, IDX_DIM), F32), jax.ShapeDtypeStruct((n, FOX_HEADS), F32),
                     jax.ShapeDtypeStruct((n, IDX_HEADS), F32))
        out_specs = (pl.BlockSpec((tm, IDX_DIM), row), pl.BlockSpec((tm, FOX_HEADS), row),
                     pl.BlockSpec((tm, IDX_HEADS), row))
    else:
        out_shape = jax.ShapeDtypeStruct((n, cols), F32)
        out_specs = pl.BlockSpec((tm, cols), row)
    return pl.pallas_call(
        functools.partial(_proj_kernel, kind=kind),
        grid=grid, in_specs=in_specs, out_specs=out_specs, out_shape=out_shape,
        compiler_params=_cparams(("arbitrary",)), name=f"proj_{kind}",
    )(*args)


def _project(x, pos_rows, w_in, b_f, tm):
    seg = lambda i: w_in[:, OFFS[i]:OFFS[i + 1]].astype(BF16)
    t128 = _rope_tables(pos_rows, HEAD_DIM)
    t64 = _rope_tables(pos_rows, IDX_DIM)
    fq = _project_segment(x, seg(0), "plain", (), tm)
    fk = _project_segment(x, seg(1), "plain", (), tm)
    fv = _project_segment(x, seg(2), "plain", (), tm)
    fg = _project_segment(x, seg(4), "plain", (), tm)
    dq = _project_segment(x, seg(5), "rope128", t128, tm)
    dk = _project_segment(x, seg(6), "rope128", t128, tm)
    dv = _project_segment(x, seg(7), "plain", (), tm)
    dg = _project_segment(x, seg(8), "plain", (), tm)
    iq = _project_segment(x, seg(9), "rope64", t64, tm)
    pad = lambda a: jnp.pad(a, ((0, 0), (0, LANES - a.shape[1])))
    w_small = jnp.concatenate([pad(w_in[:, OFFS[10]:OFFS[11]]), pad(w_in[:, OFFS[3]:OFFS[4]]),
                               pad(w_in[:, OFFS[11]:OFFS[12]])], axis=1).astype(BF16)
    ik, logf, iw = _project_segment(x, w_small, "small", t64, tm, extra=pad(b_f[None, :].astype(F32)))
    return fq, fk, fv, logf, fg, dq, dk, dv, dg, iq, ik, iw


def _cumsum_kernel(x_ref, o_ref, *, suffix):
    rows, n = x_ref.shape
    r = lax.broadcasted_iota(jnp.int32, (LANES, LANES), 0)
    c = lax.broadcasted_iota(jnp.int32, (LANES, LANES), 1)
    tri = ((r > c) if suffix else (r <= c)).astype(F32)
    carry = jnp.zeros((rows, 1), F32)
    nchunks = n // LANES
    for step in range(nchunks):
        j = nchunks - 1 - step if suffix else step
        x = x_ref[:, j * LANES:(j + 1) * LANES]
        blk = jnp.dot(x, tri, precision=lax.Precision.HIGHEST, preferred_element_type=F32) + carry
        o_ref[:, j * LANES:(j + 1) * LANES] = blk
        carry = (blk[:, 0:1] + x[:, 0:1]) if suffix else blk[:, LANES - 1:LANES]


def _cumsum_lanes(x, suffix=False):
    rows, n = x.shape
    tr = min(rows, 64)
    return pl.pallas_call(
        functools.partial(_cumsum_kernel, suffix=suffix), grid=(rows // tr,),
        in_specs=[pl.BlockSpec((tr, n), lambda i: (i, 0))], out_specs=pl.BlockSpec((tr, n), lambda i: (i, 0)),
        out_shape=jax.ShapeDtypeStruct(x.shape, F32), compiler_params=_cparams(("arbitrary",)),
        name="cumsum_suffix" if suffix else "cumsum")(x)


def _flash_kernel(qi_tab, ki_tab, q_ref, k_ref, v_ref, g_ref, *refs, mode, T, nheads):
    if mode == "fox":
        ctok_ref, crow_ref, o_ref, m_scr, l_scr, acc_scr = refs
    else:
        mask_ref, o_ref, m_scr, l_scr, acc_scr = refs
    t = pl.program_id(1)
    qi, ki = qi_tab[t], ki_tab[t]

    @pl.when(ki == 0)
    def _():
        m_scr[...] = jnp.full(m_scr.shape, MASK_NEG, F32)
        l_scr[...] = jnp.zeros(l_scr.shape, F32)
        acc_scr[...] = jnp.zeros(acc_scr.shape, F32)

    def step(diag):
        if mode == "dsa":
            keep = mask_ref[0].astype(jnp.int32) != 0
        elif diag:
            keep = (lax.broadcasted_iota(jnp.int32, (T, T), 1) <= lax.broadcasted_iota(jnp.int32, (T, T), 0))
        for h in range(nheads):
            sl = slice(h * HEAD_DIM, (h + 1) * HEAD_DIM)
            qh = q_ref[0, :, sl].astype(BF16)
            kh = k_ref[0, :, sl].astype(BF16)
            vh = v_ref[0, :, sl].astype(BF16)
            s = lax.dot_general(qh, kh, (((1,), (1,)), ((), ())), preferred_element_type=F32) * (ATTN_SCALE * LOG2E)
            if mode == "fox":
                s = s + (ctok_ref[0, :, h:h + 1] * LOG2E - crow_ref[0, h:h + 1, :] * LOG2E)
            if mode == "dsa" or diag:
                s = jnp.where(keep, s, MASK_NEG)
            m_prev = m_scr[h, :, 0:1]
            m_new = jnp.maximum(m_prev, jnp.max(s, axis=1, keepdims=True))
            p = jnp.exp2(s - m_new)
            alpha = jnp.exp2(m_prev - m_new)
            l_new = alpha * l_scr[h, :, 0:1] + jnp.sum(p, axis=1, keepdims=True)
            acc_scr[:, sl] = alpha * acc_scr[:, sl] + jnp.dot(p.astype(BF16), vh, preferred_element_type=F32)
            m_scr[h] = jnp.broadcast_to(m_new, (T, LANES))
            l_scr[h] = jnp.broadcast_to(l_new, (T, LANES))

    if mode == "dsa":
        step(False)
    else:
        pl.when(ki == qi)(functools.partial(step, True))
        pl.when(ki != qi)(functools.partial(step, False))

    @pl.when(ki == qi)
    def _():
        for h in range(nheads):
            sl = slice(h * HEAD_DIM, (h + 1) * HEAD_DIM)
            g = g_ref[0, :, sl]
            o = acc_scr[:, sl] / l_scr[h, :, 0:1]
            o_ref[0, :, sl] = (o * (g / (1.0 + jnp.exp(-g)))).astype(BF16)


def _flash(mode, q, k, v, gate, extra, T):
    b, s, w = q.shape
    nheads = w // HEAD_DIM
    nb = s // T
    pairs = [(i, j) for i in range(nb) for j in range(i + 1)]
    qi_tab = jnp.asarray([p[0] for p in pairs], jnp.int32)
    ki_tab = jnp.asarray([p[1] for p in pairs], jnp.int32)
    qmap = lambda bb, t, qt, kt: (bb, qt[t], 0)
    kmap = lambda bb, t, qt, kt: (bb, kt[t], 0)
    in_specs = [pl.BlockSpec((1, T, w), qmap), pl.BlockSpec((1, T, w), kmap), pl.BlockSpec((1, T, w), kmap),
                pl.BlockSpec((1, T, w), qmap)]
    if mode == "fox":
        c_tok, c_row = extra
        in_specs += [pl.BlockSpec((1, T, nheads), qmap),
                     pl.BlockSpec((1, nheads, T), lambda bb, t, qt, kt: (bb, 0, kt[t]))]
        args = (c_tok, c_row)
    else:
        in_specs += [pl.BlockSpec((1, T, T), lambda bb, t, qt, kt: (bb, qt[t], kt[t]))]
        args = (extra,)
    grid_spec = pltpu.PrefetchScalarGridSpec(
        num_scalar_prefetch=2, grid=(b, len(pairs)), in_specs=in_specs,
        out_specs=pl.BlockSpec((1, T, w), qmap),
        scratch_shapes=[pltpu.VMEM((nheads, T, LANES), F32), pltpu.VMEM((nheads, T, LANES), F32),
                        pltpu.VMEM((T, w), F32)])
    return pl.pallas_call(
        functools.partial(_flash_kernel, mode=mode, T=T, nheads=nheads),
        grid_spec=grid_spec, out_shape=jax.ShapeDtypeStruct((b, s, w), BF16),
        compiler_params=_cparams(("arbitrary", "arbitrary")), name=f"flash_{mode}",
    )(qi_tab, ki_tab, q, k, v, gate, *args)


def _sortable_key(x):
    bits = pltpu.bitcast(x, jnp.int32)
    return bits ^ ((bits >> 31) & 0x7FFFFFFF)


def _kth_threshold(count_ge, topk, rows):
    n0 = count_ge(jnp.zeros((rows, 1), jnp.int32))
    t0 = jnp.where(n0 >= topk, 0, INT_MIN).astype(jnp.int32)
    e0 = (n0 == topk).astype(jnp.int32)

    def cond(carry):
        i, _, _, pending = carry
        return jnp.logical_and(i < 31, pending > 0)

    def body(carry):
        i, t, exact, _ = carry
        cand = t | (jnp.int32(1) << (30 - i))
        n = count_ge(cand)
        take = jnp.logical_and(n >= topk, exact == 0)
        t = jnp.where(take, cand, t)
        exact = jnp.where(jnp.logical_and(take, n == topk), 1, exact)
        return i + 1, t, exact, jnp.sum(1 - exact)

    _, t, exact, _ = lax.while_loop(cond, body, (jnp.int32(0), t0, e0, jnp.sum(1 - e0)))
    return t, jnp.where(exact == 1, topk, -1)


def _topk_cut(count, topk, rows, ncols, jstar_scr):
    thr, n_ge_known = _kth_threshold(lambda cand: count(lambda key, col, c: key >= c, cand), topk, rows)
    real = thr > INT_MIN
    jstar_scr[...] = jnp.broadcast_to(jnp.where(real, ncols - 1, -1).astype(jnp.int32), jstar_scr.shape)

    @pl.when(jnp.min(n_ge_known) < 0)
    def _():
        n_ge = count(lambda key, col, t: key >= t, thr)
        tie = jnp.logical_and(real, n_ge > topk)

        @pl.when(jnp.max(tie.astype(jnp.int32)) > 0)
        def _():
            need = topk - count(lambda key, col, t: key > t, thr)
            nbits = max(1, (ncols - 1).bit_length())

            def body(b, j):
                cand = j | (jnp.int32(1) << (nbits - 1 - b))
                below = count(lambda key, col, t, c: jnp.logical_and(key == t, col < c), thr, cand)
                return jnp.where(below < need, cand, j)

            j = lax.fori_loop(0, nbits, body, jnp.zeros((rows, 1), jnp.int32))
            jstar_scr[...] = jnp.broadcast_to(jnp.where(tie, j, jstar_scr[:, 0:1]), jstar_scr.shape)

    return thr, jstar_scr[:, 0:1]


def _selected(key, col, thr, jstar):
    return jnp.where(key == thr, (col <= jstar).astype(jnp.int32), (key > thr).astype(jnp.int32))


def _select_kernel(iq_ref, iw_ref, ik2_ref, mask_ref, keys_scr, jstar_scr, *, tq, tk, topk, seq):
    i = pl.program_id(1)
    nch = ((i + 1) * tq + tk - 1) // tk
    iqb = iq_ref[0].astype(BF16)
    w = iw_ref[0]
    row = i * tq + lax.broadcasted_iota(jnp.int32, (tq, 1), 0)
    lane = lax.broadcasted_iota(jnp.int32, (1, tk), 1)

    def score_chunk(c, carry):
        off = pl.multiple_of(c * tk, tk)
        k2 = ik2_ref[0, :, pl.ds(pl.multiple_of(c * 2 * tk, 2 * tk), 2 * tk)]
        acc = jnp.zeros((tq, tk), F32)
        for pr in range(IDX_HEADS // 2):
            r = jnp.dot(iqb[:, pr * 128:(pr + 1) * 128], k2, preferred_element_type=F32)
            acc = acc + w[:, 2 * pr:2 * pr + 1] * jnp.maximum(r[:, :tk], 0.0)
            acc = acc + w[:, 2 * pr + 1:2 * pr + 2] * jnp.maximum(r[:, tk:], 0.0)
        key = jnp.where(off + lane <= row, _sortable_key(acc), INT_MIN)
        keys_scr[:, pl.ds(off, tk)] = key
        return carry

    lax.fori_loop(0, nch, score_chunk, 0)

    rb = min(tq, 128)

    def count(pred, *ops):
        parts = []
        for r0 in range(0, tq, rb):
            ops_r = [jnp.broadcast_to(o[r0:r0 + rb], (rb, LANES)) for o in ops]

            def body(c, acc, r0=r0, ops_r=ops_r):
                for j in range(tk // LANES):
                    off = pl.multiple_of(c * tk + j * LANES, LANES)
                    hit = pred(keys_scr[r0:r0 + rb, pl.ds(off, LANES)], off + lane[:, 0:LANES], *ops_r)
                    acc = acc + hit.astype(jnp.int32)
                return acc

            acc = lax.fori_loop(0, nch, body, jnp.zeros((rb, LANES), jnp.int32))
            parts.append(jnp.sum(acc, axis=1, keepdims=True))
        return jnp.concatenate(parts, axis=0)

    thr, jstar = _topk_cut(count, topk, tq, seq, jstar_scr)

    def emit(c, carry):
        off = pl.multiple_of(c * tk, tk)
        sel = _selected(keys_scr[:, pl.ds(off, tk)], off + lane, thr, jstar)
        mask_ref[0, :, pl.ds(off, tk)] = sel.astype(jnp.int8)
        return carry

    lax.fori_loop(0, nch, emit, 0)

    def clear(c, carry):
        mask_ref[0, :, pl.ds(pl.multiple_of(c * tk, tk), tk)] = jnp.zeros((tq, tk), jnp.int8)
        return carry

    lax.fori_loop(nch, seq // tk, clear, 0)


def _block_diag_keys(ik, tk):
    b, s, d = ik.shape
    kt = jnp.swapaxes(ik.astype(BF16), 1, 2).reshape(b, d, s // tk, tk)
    z = jnp.zeros_like(kt)
    top = jnp.stack([kt, z], axis=3)
    bot = jnp.stack([z, kt], axis=3)
    return jnp.concatenate([top, bot], axis=1).reshape(b, 2 * d, 2 * s)


def _select_prompt(iq, iw, ik, topk, tq, tk):
    b, s, _ = iq.shape
    ik2 = _block_diag_keys(ik, tk)
    return pl.pallas_call(
        functools.partial(_select_kernel, tq=tq, tk=tk, topk=topk, seq=s),
        grid=(b, s // tq),
        in_specs=[pl.BlockSpec((1, tq, IDX_HEADS * IDX_DIM), lambda bb, i: (bb, i, 0)),
                  pl.BlockSpec((1, tq, IDX_HEADS), lambda bb, i: (bb, i, 0)),
                  pl.BlockSpec((1, 2 * IDX_DIM, 2 * s), lambda bb, i: (bb, 0, 0))],
        out_specs=pl.BlockSpec((1, tq, s), lambda bb, i: (bb, i, 0)),
        out_shape=jax.ShapeDtypeStruct((b, s, s), jnp.int8),
        scratch_shapes=[pltpu.VMEM((tq, s), jnp.int32), pltpu.VMEM((tq, LANES), jnp.int32)],
        compiler_params=_cparams(("arbitrary", "arbitrary")), name="select_prompt",
    )(iq, iw, ik2)


def _outnorm_kernel(x_ref, mf_ref, md_ref, w1_ref, w2_ref, g_ref, b_ref, o_ref):
    y = jnp.dot(mf_ref[...].astype(BF16), w1_ref[...], preferred_element_type=F32)
    y = y + jnp.dot(md_ref[...].astype(BF16), w2_ref[...], preferred_element_type=F32)
    z = DEEPNORM_ALPHA * x_ref[...] + y
    mu = jnp.mean(z, axis=1, keepdims=True)
    zc = z - mu
    var = jnp.mean(zc * zc, axis=1, keepdims=True)
    o_ref[...] = zc * lax.rsqrt(var + LN_EPS) * g_ref[...] + b_ref[...]


def _output_and_norm(x, mix_fox, mix_dsa, w_out, ln_g, ln_b, tm):
    n = x.shape[0]
    w1 = w_out[:WIDTH].astype(BF16)
    w2 = w_out[WIDTH:].astype(BF16)
    row = lambda i: (i, 0)
    const = lambda i: (0, 0)
    return pl.pallas_call(
        _outnorm_kernel, grid=(n // tm,),
        in_specs=[pl.BlockSpec((tm, D_MODEL), row), pl.BlockSpec((tm, WIDTH), row), pl.BlockSpec((tm, WIDTH), row),
                  pl.BlockSpec((WIDTH, D_MODEL), const), pl.BlockSpec((WIDTH, D_MODEL), const),
                  pl.BlockSpec((1, D_MODEL), const), pl.BlockSpec((1, D_MODEL), const)],
        out_specs=pl.BlockSpec((tm, D_MODEL), row),
        out_shape=jax.ShapeDtypeStruct((n, D_MODEL), F32),
        compiler_params=_cparams(("arbitrary",)), name="outnorm",
    )(x, mix_fox, mix_dsa, w1, w2, ln_g[None, :].astype(F32), ln_b[None, :].astype(F32))


def _prompt_tiles(s):
    tm = min(512, s)
    t_attn = min(512, s)
    tq_sel = min(256, s)
    tk_sel = min(512, s)
    return tm, t_attn, tq_sel, tk_sel


def _prompt_group(x, w_in, b_f, w_out, ln_g, ln_b):
    b, s, d = x.shape
    tm, t_attn, tq_sel, tk_sel = _prompt_tiles(s)
    x2 = x.reshape(b * s, d)
    fq, fk, fv, logf, fg, dq, dk, dv, dg, iq, ik, iw = _project(x2, jnp.arange(s), w_in, b_f, tm)
    r3 = lambda a: a.reshape(b, s, a.shape[-1])
    c_row = _cumsum_lanes(jnp.swapaxes(r3(logf), 1, 2).reshape(b * FOX_HEADS, s)).reshape(b, FOX_HEADS, s)
    c_tok = jnp.swapaxes(c_row, 1, 2)
    mix_fox = _flash("fox", r3(fq), r3(fk), r3(fv), r3(fg), (c_tok, c_row), t_attn)
    topk = min(TOPK_MAX, s // 4)
    mask = _select_prompt(r3(iq), r3(iw), r3(ik), topk, tq_sel, tk_sel)
    mix_dsa = _flash("dsa", r3(dq), r3(dk), r3(dv), r3(dg), mask, t_attn)
    y = _output_and_norm(x2, mix_fox.reshape(b * s, WIDTH), mix_dsa.reshape(b * s, WIDTH), w_out, ln_g, ln_b, tm)
    heads = lambda a, h: a.reshape(b, 1, s, h, HEAD_DIM)
    return (y.reshape(b, s, d), heads(fk, FOX_HEADS), heads(fv, FOX_HEADS), logf.reshape(b, 1, s, FOX_HEADS),
            heads(dk, DSA_HEADS), heads(dv, DSA_HEADS), ik.reshape(b, 1, s, IDX_DIM))


def _page_specs(block, npages_per_step, n_pages):
    zeros = (0,) * (len(block) - 1)
    return [pl.BlockSpec(block, lambda b, s, pt, g=g: (pt[b * n_pages + s * npages_per_step + g],) + zeros)
            for g in range(npages_per_step)]


def _gather_rows_kernel(pt_ref, *refs):
    pages, o_ref = refs[:-1], refs[-1]
    for g, page in enumerate(pages):
        o_ref[0, g:g + 1, :] = page[0]


def _gather_rows(cache_rows, pt_flat, db, n_pages, gp):
    w = cache_rows.shape[-1]
    grid_spec = pltpu.PrefetchScalarGridSpec(
        num_scalar_prefetch=1, grid=(db, n_pages // gp), in_specs=_page_specs((1, 1, w), gp, n_pages),
        out_specs=pl.BlockSpec((1, gp, w), lambda b, s, pt: (b, s, 0)))
    return pl.pallas_call(
        _gather_rows_kernel, grid_spec=grid_spec, out_shape=jax.ShapeDtypeStruct((db, n_pages, w), F32),
        compiler_params=_cparams(("arbitrary", "arbitrary")), name="gather_rows",
    )(pt_flat, *([cache_rows] * gp))


def _dot_exact(x, binary):
    hi = x.astype(BF16)
    r1 = x - hi.astype(F32)
    mid = r1.astype(BF16)
    lo = (r1 - mid.astype(F32)).astype(BF16)
    return (jnp.dot(hi, binary, preferred_element_type=F32) + jnp.dot(mid, binary, preferred_element_type=F32)
            + jnp.dot(lo, binary, preferred_element_type=F32))


def _past_decay_kernel(l_ref, within_ref, total_ref, o_ref):
    x = l_ref[0]
    n_pages = x.shape[0]
    within = _dot_exact(x, within_ref[...])
    totals = _dot_exact(x, total_ref[...])
    later = (lax.broadcasted_iota(jnp.int32, (n_pages, n_pages), 1)
             > lax.broadcasted_iota(jnp.int32, (n_pages, n_pages), 0)).astype(BF16)
    hi = totals.astype(BF16)
    r1 = totals - hi.astype(F32)
    mid = r1.astype(BF16)
    lo = (r1 - mid.astype(F32)).astype(BF16)
    across = (jnp.dot(later, hi, preferred_element_type=F32) + jnp.dot(later, mid, preferred_element_type=F32)
              + jnp.dot(later, lo, preferred_element_type=F32))
    o_ref[0] = within + across


def _past_decay(past_logf):
    db, n_pages, w = past_logf.shape
    i = jnp.arange(w)
    same_head = (i[:, None] % FOX_HEADS) == (i[None, :] % FOX_HEADS)
    within = jnp.logical_and(same_head, i[:, None] > i[None, :]).astype(BF16)
    total = same_head.astype(BF16)
    const = lambda b: (0, 0)
    per_b = pl.BlockSpec((1, n_pages, w), lambda b: (b, 0, 0))
    return pl.pallas_call(
        _past_decay_kernel, grid=(db,),
        in_specs=[per_b, pl.BlockSpec((w, w), const), pl.BlockSpec((w, w), const)], out_specs=per_b,
        out_shape=jax.ShapeDtypeStruct((db, n_pages, w), F32),
        compiler_params=_cparams(("arbitrary",)), name="past_decay",
    )(past_logf, within, total)


def _select_sample_kernel(pt_ref, iqm_ref, wcol_ref, iknew_ref, *refs, gs, topk, past, T):
    pages = refs[:gs]
    mp_ref, mn_ref, keys_scr, jstar_scr = refs[gs:]
    s_idx = pl.program_id(1)
    iqm = iqm_ref[0]
    wcol = wcol_ref[0]
    ncols = past + PAGE_SIZE

    def scores(kblk):
        r = lax.dot_general(iqm, kblk.astype(BF16), (((1,), (1,)), ((), ())), preferred_element_type=F32)
        r = wcol * jnp.maximum(r, 0.0)
        acc = r[0:T]
        for h in range(1, IDX_HEADS):
            acc = acc + r[h * T:(h + 1) * T]
        return acc

    for g in range(gs):
        off = pl.multiple_of((s_idx * gs + g) * PAGE_SIZE, PAGE_SIZE)
        keys_scr[:, pl.ds(off, PAGE_SIZE)] = _sortable_key(scores(pages[g][0]))

    @pl.when(s_idx == pl.num_programs(1) - 1)
    def _():
        t_col = lax.broadcasted_iota(jnp.int32, (T, 1), 0)
        j_row = lax.broadcasted_iota(jnp.int32, (1, PAGE_SIZE), 1)
        keys_scr[:, past:ncols] = jnp.where(j_row <= t_col, _sortable_key(scores(iknew_ref[0])), INT_MIN)
        col = lax.broadcasted_iota(jnp.int32, (1, ncols), 1)

        def count(pred, *ops):
            return jnp.sum(pred(keys_scr[...], col, *ops).astype(jnp.int32), axis=1, keepdims=True)

        thr, jstar = _topk_cut(count, topk, T, ncols, jstar_scr)
        sel = _selected(keys_scr[...], col, thr, jstar).astype(F32)
        for pg in range(past // PAGE_SIZE):
            mp_ref[0, pg] = sel[:, pg * PAGE_SIZE:(pg + 1) * PAGE_SIZE]
        mn_ref[0] = sel[:, past:]


def _select_sample(iq, iw, ik_new, cache_ik, pt_flat, n_pages, topk, gs):
    db, T, _ = iq.shape
    past = n_pages * PAGE_SIZE
    rows = IDX_HEADS * T
    iqm = jnp.swapaxes(iq.reshape(db, T, IDX_HEADS, IDX_DIM), 1, 2).reshape(db, rows, IDX_DIM).astype(BF16)
    wcol = jnp.swapaxes(iw, 1, 2).reshape(db, rows, 1)
    ik_pad = jnp.pad(ik_new, ((0, 0), (0, PAGE_SIZE - T), (0, 0)))
    per_b = lambda blk: pl.BlockSpec(blk, lambda b, s, pt: (b, 0, 0))
    grid_spec = pltpu.PrefetchScalarGridSpec(
        num_scalar_prefetch=1, grid=(db, n_pages // gs),
        in_specs=[per_b((1, rows, IDX_DIM)), per_b((1, rows, 1)), per_b((1, PAGE_SIZE, IDX_DIM))]
        + _page_specs((1, PAGE_SIZE, IDX_DIM), gs, n_pages),
        out_specs=(pl.BlockSpec((1, n_pages, T, PAGE_SIZE), lambda b, s, pt: (b, 0, 0, 0)), per_b((1, T, PAGE_SIZE))),
        scratch_shapes=[pltpu.VMEM((T, past + PAGE_SIZE), jnp.int32), pltpu.VMEM((T, LANES), jnp.int32)])
    return pl.pallas_call(
        functools.partial(_select_sample_kernel, gs=gs, topk=topk, past=past, T=T), grid_spec=grid_spec,
        out_shape=(jax.ShapeDtypeStruct((db, n_pages, T, PAGE_SIZE), F32),
                   jax.ShapeDtypeStruct((db, T, PAGE_SIZE), F32)),
        compiler_params=_cparams(("arbitrary", "arbitrary")), name="select_sample",
    )(pt_flat, iqm, wcol, ik_pad, *([cache_ik] * gs))


def _decode_kernel(pt_ref, q_ref, knew_ref, vnew_ref, gate_ref, *refs, mode, G, nheads, T):
    if mode == "fox":
        pd_ref, lfcol_ref, lfrow_ref = refs[:3]
        refs = refs[3:]
    else:
        mp_ref, mn_ref = refs[:2]
        refs = refs[2:]
    k_refs, v_refs = refs[:G], refs[G:2 * G]
    o_ref, m_scr, l_scr, acc_scr = refs[2 * G:2 * G + 4]
    s_idx = pl.program_id(1)
    rows = nheads * T
    vkeys = PAGE_SIZE * nheads
    nt = (((1,), (1,)), ((), ()))

    @pl.when(s_idx == 0)
    def _():
        m_scr[...] = jnp.full(m_scr.shape, MASK_NEG, F32)
        l_scr[...] = jnp.zeros(l_scr.shape, F32)
        acc_scr[...] = jnp.zeros(acc_scr.shape, F32)

    q = q_ref[0]
    row_head = lax.broadcasted_iota(jnp.int32, (rows, 1), 0) // T
    row_tok = lax.broadcasted_iota(jnp.int32, (rows, 1), 0) % T
    lane = lax.broadcasted_iota(jnp.int32, (1, vkeys), 1)
    own = jnp.where((lane % nheads) == row_head, 0.0, MASK_NEG)
    if mode == "fox":
        ri = lax.broadcasted_iota(jnp.int32, (rows, rows), 0)
        ci = lax.broadcasted_iota(jnp.int32, (rows, rows), 1)
        tri_col = jnp.logical_and(ri // T == ci // T, ci % T <= ri % T).astype(F32)
        nc_col = jnp.dot(tri_col, lfcol_ref[0], precision=lax.Precision.HIGHEST, preferred_element_type=F32)
        ui = lax.broadcasted_iota(jnp.int32, (LANES, LANES), 0)
        uj = lax.broadcasted_iota(jnp.int32, (LANES, LANES), 1)
        tri_row = jnp.logical_and(jnp.logical_and(ui % nheads == uj % nheads, ui <= uj), uj < rows).astype(F32)
        nc_row = jnp.dot(lfrow_ref[0], tri_row, precision=lax.Precision.HIGHEST, preferred_element_type=F32)
        own_nc = own + nc_col[:, 0:1]
    else:
        expand = (lax.broadcasted_iota(jnp.int32, (PAGE_SIZE, vkeys), 1) // nheads
                  == lax.broadcasted_iota(jnp.int32, (PAGE_SIZE, vkeys), 0)).astype(BF16)
        picked = jnp.dot(mp_ref[0].reshape(G * T, PAGE_SIZE).astype(BF16), expand, preferred_element_type=F32)

    def update(s, vals):
        width = vals[0].shape[0]
        m_prev = m_scr[:, 0:1]
        m_new = jnp.maximum(m_prev, jnp.max(s, axis=1, keepdims=True))
        p = jnp.exp(s - m_new)
        alpha = jnp.exp(m_prev - m_new)
        l_scr[...] = jnp.broadcast_to(alpha * l_scr[:, 0:1] + jnp.sum(p, axis=1, keepdims=True), l_scr.shape)
        p = p.astype(BF16)
        pv = jnp.dot(p[:, 0:width], vals[0], preferred_element_type=F32)
        for g in range(1, len(vals)):
            pv = pv + jnp.dot(p[:, g * width:(g + 1) * width], vals[g], preferred_element_type=F32)
        acc_scr[...] = alpha * acc_scr[...] + pv
        m_scr[...] = jnp.broadcast_to(m_new, m_scr.shape)

    parts = []
    for g in range(G):
        s = lax.dot_general(q, k_refs[g][0].astype(BF16), nt, preferred_element_type=F32) * ATTN_SCALE
        if mode == "fox":
            s = s + (own_nc + pd_ref[0, g:g + 1, :])
        else:
            keep = jnp.concatenate([picked[g * T:(g + 1) * T]] * nheads, axis=0) > 0.5
            s = jnp.where(keep, s + own, MASK_NEG)
        parts.append(s)
    update(jnp.concatenate(parts, axis=1), [v[0].astype(BF16) for v in v_refs])

    @pl.when(s_idx == pl.num_programs(1) - 1)
    def _():
        sn = lax.dot_general(q, knew_ref[0].astype(BF16), nt, preferred_element_type=F32) * ATTN_SCALE
        lane_n = lax.broadcasted_iota(jnp.int32, (1, LANES), 1)
        if mode == "fox":
            keep = (lane_n // nheads) <= row_tok
            sn = sn + (nc_col[:, 0:1] - nc_row[0:1, :])
        else:
            keep = jnp.concatenate([jnp.dot(mn_ref[0].astype(BF16), expand[:, 0:LANES],
                                            preferred_element_type=F32)] * nheads, axis=0) > 0.5
        sn = jnp.where(keep, sn + own[:, 0:LANES], MASK_NEG)
        update(sn, [vnew_ref[0].astype(BF16)])
        o = acc_scr[...] / l_scr[:, 0:1]
        for h in range(nheads):
            sl = slice(h * HEAD_DIM, (h + 1) * HEAD_DIM)
            g = gate_ref[0, :, sl]
            o_ref[0, :, sl] = o[h * T:(h + 1) * T] * (g / (1.0 + jnp.exp(-g)))


def _decode(mode, q, k_new, v_new, gate, extra, cache_k, cache_v, pt_flat, n_pages, G):
    db, T, w = q.shape
    nheads = w // HEAD_DIM
    rows = nheads * T
    assert rows <= LANES and nheads * HEAD_DIM == w
    vkeys = PAGE_SIZE * nheads
    q_rows = jnp.swapaxes(q.reshape(db, T, nheads, HEAD_DIM), 1, 2).reshape(db, rows, HEAD_DIM).astype(BF16)
    new_rows = lambda a: jnp.pad(a.reshape(db, rows, HEAD_DIM), ((0, 0), (0, LANES - rows), (0, 0)))
    per_b = lambda blk: pl.BlockSpec(blk, lambda b, s, pt: (b,) + (0,) * (len(blk) - 1))
    step = lambda blk: pl.BlockSpec(blk, lambda b, s, pt: (b, s) + (0,) * (len(blk) - 2))
    in_specs = [per_b((1, rows, HEAD_DIM)), per_b((1, LANES, HEAD_DIM)), per_b((1, LANES, HEAD_DIM)),
                per_b((1, T, w))]
    if mode == "fox":
        pd, logf_new = extra
        lf_col = jnp.broadcast_to(jnp.swapaxes(logf_new, 1, 2).reshape(db, rows, 1), (db, rows, LANES))
        lf_row = jnp.broadcast_to(jnp.pad(logf_new.reshape(db, 1, rows), ((0, 0), (0, 0), (0, LANES - rows))),
                                  (db, 8, LANES))
        in_specs += [step((1, G, vkeys)), per_b((1, rows, LANES)), per_b((1, 8, LANES))]
        args = (pd, lf_col, lf_row)
    else:
        mask_past, mask_new = extra
        in_specs += [step((1, G, T, PAGE_SIZE)), per_b((1, T, PAGE_SIZE))]
        args = (mask_past, mask_new)
    in_specs += _page_specs((1, vkeys, HEAD_DIM), G, n_pages) + _page_specs((1, vkeys, HEAD_DIM), G, n_pages)
    grid_spec = pltpu.PrefetchScalarGridSpec(
        num_scalar_prefetch=1, grid=(db, n_pages // G), in_specs=in_specs, out_specs=per_b((1, T, w)),
        scratch_shapes=[pltpu.VMEM((rows, LANES), F32), pltpu.VMEM((rows, LANES), F32),
                        pltpu.VMEM((rows, HEAD_DIM), F32)])
    return pl.pallas_call(
        functools.partial(_decode_kernel, mode=mode, G=G, nheads=nheads, T=T), grid_spec=grid_spec,
        out_shape=jax.ShapeDtypeStruct((db, T, w), F32),
        compiler_params=_cparams(("arbitrary", "arbitrary")), name=f"decode_{mode}",
    )(pt_flat, q_rows, new_rows(k_new), new_rows(v_new), gate, *args, *([cache_k] * G), *([cache_v] * G))


def _sample_group(x, caches, page_table, w_in, b_f, w_out, ln_g, ln_b):
    cache_fox_k, cache_fox_v, cache_fox_logf, cache_dsa_k, cache_dsa_v, cache_idx_k = caches
    db, T, d = x.shape
    n_pages = page_table.shape[1]
    past = n_pages * PAGE_SIZE
    nphys = cache_fox_k.shape[0]
    G = min(8, n_pages)
    pt_flat = page_table.reshape(-1).astype(jnp.int32)
    x2 = x.reshape(db * T, d)
    pos = jnp.tile(past + jnp.arange(T), db)
    fq, fk, fv, logf, fg, dq, dk, dv, dg, iq, ik, iw = _project(x2, pos, w_in, b_f, db * T)
    r3 = lambda a: a.reshape(db, T, a.shape[-1])
    kv_rows = lambda c: c.reshape(nphys, PAGE_SIZE * c.shape[3], HEAD_DIM)
    past_logf = _gather_rows(cache_fox_logf.reshape(nphys, 1, PAGE_SIZE * FOX_HEADS), pt_flat, db, n_pages,
                             min(32, n_pages))
    pd = _past_decay(past_logf)
    mix_fox = _decode("fox", r3(fq), r3(fk), r3(fv), r3(fg), (pd, r3(logf)),
                      kv_rows(cache_fox_k), kv_rows(cache_fox_v), pt_flat, n_pages, G)
    topk = min(TOPK_MAX, (past + T) // 4)
    masks = _select_sample(r3(iq), r3(iw), r3(ik), cache_idx_k.reshape(nphys, PAGE_SIZE, IDX_DIM), pt_flat, n_pages,
                           topk, min(16, n_pages))
    mix_dsa = _decode("dsa", r3(dq), r3(dk), r3(dv), r3(dg), masks,
                      kv_rows(cache_dsa_k), kv_rows(cache_dsa_v), pt_flat, n_pages, G)
    y = _output_and_norm(x2, mix_fox.reshape(db * T, WIDTH), mix_dsa.reshape(db * T, WIDTH), w_out, ln_g, ln_b,
                         db * T)
    heads = lambda a, h: a.reshape(db, 1, T, h, HEAD_DIM)
    return (y.reshape(db, T, d), heads(fk, FOX_HEADS), heads(fv, FOX_HEADS), logf.reshape(db, 1, T, FOX_HEADS),
            heads(dk, DSA_HEADS), heads(dv, DSA_HEADS), ik.reshape(db, 1, T, IDX_DIM))


def kernel(x_prompt, x_sample, cache_fox_k, cache_fox_v, cache_fox_logf, cache_dsa_k, cache_dsa_v, cache_idx_k,
           page_table, w_in, b_f, w_out, ln_g, ln_b):
    assert w_in.shape[0] == DEPTH and cache_fox_k.shape[1] == DEPTH
    params = (w_in[0], b_f[0], w_out[0], ln_g[0], ln_b[0])
    caches = (cache_fox_k, cache_fox_v, cache_fox_logf, cache_dsa_k, cache_dsa_v, cache_idx_k)
    p = _prompt_group(x_prompt, *params)
    s = _sample_group(x_sample, caches, page_table, *params)
    return (p[0], s[0], *p[1:], *s[1:])
```

```python
import functools
import math

import jax
import jax.numpy as jnp
import numpy as np
from jax import lax
from jax.experimental import pallas as pl
from jax.experimental.pallas import tpu as pltpu

F32 = jnp.float32
BF16 = jnp.bfloat16

D_MODEL = 2048
HEAD_DIM = 128
FOX_HEADS = 8
DSA_HEADS = 8
WIDTH = FOX_HEADS * HEAD_DIM
IDX_HEADS = 16
IDX_DIM = 64
TOPK_MAX = 256
PAGE_SIZE = 128
ROPE_THETA = 10000.0
LN_EPS = 1e-5
ATTN_SCALE = HEAD_DIM ** -0.5
LOG2E = math.log2(math.e)
IDX_W_SCALE = IDX_HEADS ** -0.5 * IDX_DIM ** -0.5
DEPTH = 1
DEEPNORM_ALPHA = (2.0 * DEPTH) ** 0.25
WIDTHS = (WIDTH, WIDTH, WIDTH, FOX_HEADS, WIDTH, WIDTH, WIDTH, WIDTH, WIDTH,
          IDX_HEADS * IDX_DIM, IDX_DIM, IDX_HEADS)
OFFS = tuple(int(v) for v in np.cumsum((0,) + WIDTHS))

LANES = 128
VMEM_LIMIT_BYTES = 56 * 1024 * 1024
MASK_NEG = -1e30
SOFTMAX_ROWS = 64
HEAD_GROUP = 4
INT_MIN = -(2 ** 31)


def _cparams(sem):
    return pltpu.CompilerParams(dimension_semantics=sem, vmem_limit_bytes=VMEM_LIMIT_BYTES)


def _rope_tables(pos, dim):
    half = dim // 2
    inv = ROPE_THETA ** (-jnp.arange(half, dtype=F32) * 2.0 / dim)
    ang = pos.astype(F32)[:, None] * inv[None, :]
    cos, sin = jnp.cos(ang), jnp.sin(ang)
    zero = jnp.zeros_like(sin)
    if dim == LANES:
        return (jnp.concatenate([cos, cos], 1), jnp.concatenate([-sin, sin], 1))
    return (jnp.concatenate([cos, cos, cos, cos], 1),
            jnp.concatenate([-sin, zero, -sin, zero], 1),
            jnp.concatenate([zero, sin, zero, sin], 1))


def _rope128(blk, cos, sin):
    return blk * cos + pltpu.roll(blk, 64, axis=1) * sin


def _rope64(blk, cos, sin_lo, sin_hi):
    return blk * cos + pltpu.roll(blk, 96, axis=1) * sin_lo + pltpu.roll(blk, 32, axis=1) * sin_hi


def _proj_kernel(x_ref, w_ref, *refs, kind):
    xb = x_ref[...].astype(BF16)
    if kind == "plain":
        (o_ref,) = refs
        for c in range(0, WIDTH, 256):
            o_ref[:, c:c + 256] = jnp.dot(xb, w_ref[:, c:c + 256], preferred_element_type=F32)
    elif kind == "rope128":
        cos_ref, sin_ref, o_ref = refs
        cos, sin = cos_ref[...], sin_ref[...]
        for c in range(0, WIDTH, 256):
            acc = jnp.dot(xb, w_ref[:, c:c + 256], preferred_element_type=F32)
            for h in range(2):
                o_ref[:, c + h * 128:c + (h + 1) * 128] = _rope128(acc[:, h * 128:(h + 1) * 128], cos, sin)
    elif kind == "rope64":
        cos_ref, slo_ref, shi_ref, o_ref = refs
        cos, slo, shi = cos_ref[...], slo_ref[...], shi_ref[...]
        for c in range(0, WIDTH, 256):
            acc = jnp.dot(xb, w_ref[:, c:c + 256], preferred_element_type=F32)
            for h in range(2):
                o_ref[:, c + h * 128:c + (h + 1) * 128] = _rope64(acc[:, h * 128:(h + 1) * 128], cos, slo, shi)
    else:
        cos_ref, slo_ref, shi_ref, bf_ref, ik_ref, logf_ref, iw_ref = refs
        acc = jnp.dot(xb, w_ref[...], preferred_element_type=F32)
        ik = _rope64(acc[:, 0:128], cos_ref[...], slo_ref[...], shi_ref[...])
        ik_ref[...] = ik[:, :IDX_DIM]
        z = acc[:, 128:256] + bf_ref[...]
        logf = jnp.minimum(z, 0.0) - jnp.log(1.0 + jnp.exp(-jnp.abs(z)))
        logf_ref[...] = logf[:, :FOX_HEADS]
        iw_ref[...] = (acc[:, 256:384] * IDX_W_SCALE)[:, :IDX_HEADS]


def _project_segment(x, w, kind, tables, tm, extra=None):
    n = x.shape[0]
    cols = w.shape[1]
    grid = (n // tm,)
    tab_blocks = tables[0].shape[0] // tm if tables else 1
    row = lambda i: (i, 0)
    tab = lambda i: (i % tab_blocks, 0)
    in_specs = [pl.BlockSpec((tm, D_MODEL), row), pl.BlockSpec((D_MODEL, cols), lambda i: (0, 0))]
    in_specs += [pl.BlockSpec((tm, LANES), tab) for _ in tables]
    args = [x, w, *tables]
    if kind == "small":
        in_specs.append(pl.BlockSpec((1, LANES), lambda i: (0, 0)))
        args.append(extra)
        out_shape = (jax.ShapeDtypeStruct((n, IDX_DIM), F32), jax.ShapeDtypeStruct((n, FOX_HEADS), F32),
                     jax.ShapeDtypeStruct((n, IDX_HEADS), F32))
        out_specs = (pl.BlockSpec((tm, IDX_DIM), row), pl.BlockSpec((tm, FOX_HEADS), row),
                     pl.BlockSpec((tm, IDX_HEADS), row))
    else:
        out_shape = jax.ShapeDtypeStruct((n, cols), F32)
        out_specs = pl.BlockSpec((tm, cols), row)
    return pl.pallas_call(
        functools.partial(_proj_kernel, kind=kind),
        grid=grid, in_specs=in_specs, out_specs=out_specs, out_shape=out_shape,
        compiler_params=_cparams(("arbitrary",)), name=f"proj_{kind}",
    )(*args)


def _project(x, pos_rows, w_in, b_f, tm):
    seg = lambda i: w_in[:, OFFS[i]:OFFS[i + 1]].astype(BF16)
    t128 = _rope_tables(pos_rows, HEAD_DIM)
    t64 = _rope_tables(pos_rows, IDX_DIM)
    fq = _project_segment(x, seg(0), "plain", (), tm)
    fk = _project_segment(x, seg(1), "plain", (), tm)
    fv = _project_segment(x, seg(2), "plain", (), tm)
    fg = _project_segment(x, seg(4), "plain", (), tm)
    dq = _project_segment(x, seg(5), "rope128", t128, tm)
    dk = _project_segment(x, seg(6), "rope128", t128, tm)
    dv = _project_segment(x, seg(7), "plain", (), tm)
    dg = _project_segment(x, seg(8), "plain", (), tm)
    iq = _project_segment(x, seg(9), "rope64", t64, tm)
    pad = lambda a: jnp.pad(a, ((0, 0), (0, LANES - a.shape[1])))
    w_small = jnp.concatenate([pad(w_in[:, OFFS[10]:OFFS[11]]), pad(w_in[:, OFFS[3]:OFFS[4]]),
                               pad(w_in[:, OFFS[11]:OFFS[12]])], axis=1).astype(BF16)
    ik, logf, iw = _project_segment(x, w_small, "small", t64, tm, extra=pad(b_f[None, :].astype(F32)))
    return fq, fk, fv, logf, fg, dq, dk, dv, dg, iq, ik, iw


def _cumsum_kernel(x_ref, o_ref, *, suffix):
    rows, n = x_ref.shape
    r = lax.broadcasted_iota(jnp.int32, (LANES, LANES), 0)
    c = lax.broadcasted_iota(jnp.int32, (LANES, LANES), 1)
    tri = ((r > c) if suffix else (r <= c)).astype(F32)
    carry = jnp.zeros((rows, 1), F32)
    nchunks = n // LANES
    for step in range(nchunks):
        j = nchunks - 1 - step if suffix else step
        x = x_ref[:, j * LANES:(j + 1) * LANES]
        blk = jnp.dot(x, tri, precision=lax.Precision.HIGHEST, preferred_element_type=F32) + carry
        o_ref[:, j * LANES:(j + 1) * LANES] = blk
        carry = (blk[:, 0:1] + x[:, 0:1]) if suffix else blk[:, LANES - 1:LANES]


def _cumsum_lanes(x, suffix=False):
    rows, n = x.shape
    tr = min(rows, 64)
    return pl.pallas_call(
        functools.partial(_cumsum_kernel, suffix=suffix), grid=(rows // tr,),
        in_specs=[pl.BlockSpec((tr, n), lambda i: (i, 0))], out_specs=pl.BlockSpec((tr, n), lambda i: (i, 0)),
        out_shape=jax.ShapeDtypeStruct(x.shape, F32), compiler_params=_cparams(("arbitrary",)),
        name="cumsum_suffix" if suffix else "cumsum")(x)


def _flash_kernel(qi_tab, ki_tab, q_ref, k_ref, v_ref, g_ref, *refs, mode, T, nheads):
    if mode == "fox":
        ctok_ref, crow_ref, o_ref, m_scr, acc_scr, bias_scr = refs
    else:
        mask_ref, o_ref, m_scr, acc_scr, bias_scr = refs
    t = pl.program_id(1)
    qi, ki = qi_tab[t], ki_tab[t]

    @pl.when(ki == 0)
    def _():
        m_scr[...] = jnp.full(m_scr.shape, MASK_NEG, F32)
        acc_scr[...] = jnp.zeros(acc_scr.shape, F32)

    nt = (((1,), (1,)), ((), ()))
    subs = [slice(r, r + SOFTMAX_ROWS) for r in range(0, T, SOFTMAX_ROWS)]

    def step(diag):
        if mode == "dsa":
            for rs in subs:
                bias_scr[rs] = jnp.where(mask_ref[0, rs].astype(jnp.int32) != 0, 0.0, MASK_NEG)
        elif diag:
            for rs in subs:
                col = lax.broadcasted_iota(jnp.int32, (SOFTMAX_ROWS, T), 1)
                row = rs.start + lax.broadcasted_iota(jnp.int32, (SOFTMAX_ROWS, T), 0)
                bias_scr[rs] = jnp.where(col <= row, 0.0, MASK_NEG)
        biased = mode == "dsa" or diag

        for h0 in range(0, nheads, HEAD_GROUP):
            heads = range(h0, min(h0 + HEAD_GROUP, nheads))
            cols = {h: slice(h * HEAD_DIM, (h + 1) * HEAD_DIM) for h in heads}
            s = {h: lax.dot_general(q_ref[0, :, cols[h]].astype(BF16), k_ref[0, :, cols[h]].astype(BF16), nt,
                                    preferred_element_type=F32) * (ATTN_SCALE * LOG2E) for h in heads}
            if mode == "fox":
                s = {h: s[h] + (ctok_ref[0, :, h:h + 1] * LOG2E - crow_ref[0, h:h + 1, :] * LOG2E) for h in heads}
            if biased:
                s = {h: s[h] + bias_scr[...] for h in heads}
            m_prev = {h: m_scr[h] for h in heads}
            m_new = {h: jnp.maximum(m_prev[h], jnp.broadcast_to(jnp.max(s[h], axis=1, keepdims=True), (T, LANES)))
                     for h in heads}
            p = {h: jnp.exp2(s[h] - jnp.concatenate([m_new[h]] * (T // LANES), axis=1)).astype(BF16) for h in heads}
            alpha = {h: jnp.exp2(m_prev[h] - m_new[h]) for h in heads}
            ones = jnp.ones((T, HEAD_DIM), BF16)
            pv = {h: jnp.dot(p[h], jnp.concatenate([v_ref[0, :, cols[h]].astype(BF16), ones], axis=1),
                             preferred_element_type=F32) for h in heads}
            for h in heads:
                acc_scr[h] = jnp.concatenate([alpha[h]] * 2, axis=1) * acc_scr[h] + pv[h]
                m_scr[h] = m_new[h]

    if mode == "dsa":
        step(False)
    else:
        pl.when(ki == qi)(functools.partial(step, True))
        pl.when(ki != qi)(functools.partial(step, False))

    @pl.when(ki == qi)
    def _():
        for h in range(nheads):
            sl = slice(h * HEAD_DIM, (h + 1) * HEAD_DIM)
            g = g_ref[0, :, sl]
            o = acc_scr[h, :, 0:HEAD_DIM] / acc_scr[h, :, HEAD_DIM:2 * HEAD_DIM]
            o_ref[0, :, sl] = (o * (g / (1.0 + jnp.exp(-g)))).astype(BF16)


def _flash(mode, q, k, v, gate, extra, T):
    b, s, w = q.shape
    nheads = w // HEAD_DIM
    nb = s // T
    pairs = [(i, j) for i in range(nb) for j in range(i + 1)]
    qi_tab = jnp.asarray([p[0] for p in pairs], jnp.int32)
    ki_tab = jnp.asarray([p[1] for p in pairs], jnp.int32)
    qmap = lambda bb, t, qt, kt: (bb, qt[t], 0)
    kmap = lambda bb, t, qt, kt: (bb, kt[t], 0)
    in_specs = [pl.BlockSpec((1, T, w), qmap), pl.BlockSpec((1, T, w), kmap), pl.BlockSpec((1, T, w), kmap),
                pl.BlockSpec((1, T, w), qmap)]
    if mode == "fox":
        c_tok, c_row = extra
        in_specs += [pl.BlockSpec((1, T, nheads), qmap),
                     pl.BlockSpec((1, nheads, T), lambda bb, t, qt, kt: (bb, 0, kt[t]))]
        args = (c_tok, c_row)
    else:
        in_specs += [pl.BlockSpec((1, T, T), lambda bb, t, qt, kt: (bb, qt[t], kt[t]))]
        args = (extra,)
    grid_spec = pltpu.PrefetchScalarGridSpec(
        num_scalar_prefetch=2, grid=(b, len(pairs)), in_specs=in_specs,
        out_specs=pl.BlockSpec((1, T, w), qmap),
        scratch_shapes=[pltpu.VMEM((nheads, T, LANES), F32), pltpu.VMEM((nheads, T, 2 * HEAD_DIM), F32),
                        pltpu.VMEM((T, T), F32)])
    return pl.pallas_call(
        functools.partial(_flash_kernel, mode=mode, T=T, nheads=nheads),
        grid_spec=grid_spec, out_shape=jax.ShapeDtypeStruct((b, s, w), BF16),
        compiler_params=_cparams(("arbitrary", "arbitrary")), name=f"flash_{mode}",
    )(qi_tab, ki_tab, q, k, v, gate, *args)


def _sortable_key(x):
    bits = pltpu.bitcast(x, jnp.int32)
    return bits ^ ((bits >> 31) & 0x7FFFFFFF)


def _group_bounds(group_max):
    return jnp.min(group_max, axis=1, keepdims=True), jnp.max(group_max, axis=1, keepdims=True)


def _kth_threshold(count_ge, topk, lo, hi):
    def cond(carry):
        return carry[-1] > 0

    def body(carry):
        lo, hi, exact, _ = carry
        mid = (lo | hi) - ((lo ^ hi) >> 1)
        n = count_ge(mid)
        live = jnp.logical_and(exact == 0, lo < hi)
        up = jnp.logical_and(live, n >= topk)
        lo = jnp.where(up, mid, lo)
        hi = jnp.where(jnp.logical_and(live, n < topk), mid - 1, hi)
        exact = jnp.where(jnp.logical_and(up, n == topk), 1, exact)
        return lo, hi, exact, jnp.sum(jnp.logical_and(exact == 0, lo < hi).astype(jnp.int32))

    e0 = jnp.zeros(lo.shape, jnp.int32)
    lo, _, exact, _ = lax.while_loop(cond, body, (lo, hi, e0, jnp.sum((lo < hi).astype(jnp.int32))))
    return lo, jnp.where(exact == 1, topk, -1)


def _topk_cut(count, topk, lo, hi, ncols, jstar_scr):
    rows = lo.shape[0]
    thr, n_ge_known = _kth_threshold(lambda cand: count(lambda key, col, c: key >= c, cand), topk, lo, hi)
    real = thr > INT_MIN
    jstar_scr[...] = jnp.broadcast_to(jnp.where(real, ncols - 1, -1).astype(jnp.int32), jstar_scr.shape)

    @pl.when(jnp.min(n_ge_known) < 0)
    def _():
        n_ge = count(lambda key, col, t: key >= t, thr)
        tie = jnp.logical_and(real, n_ge > topk)

        @pl.when(jnp.max(tie.astype(jnp.int32)) > 0)
        def _():
            need = topk - count(lambda key, col, t: key > t, thr)
            nbits = max(1, (ncols - 1).bit_length())

            def body(b, j):
                cand = j | (jnp.int32(1) << (nbits - 1 - b))
                below = count(lambda key, col, t, c: jnp.logical_and(key == t, col < c), thr, cand)
                return jnp.where(below < need, cand, j)

            j = lax.fori_loop(0, nbits, body, jnp.zeros((rows, 1), jnp.int32))
            jstar_scr[...] = jnp.broadcast_to(jnp.where(tie, j, jstar_scr[:, 0:1]), jstar_scr.shape)

    return thr, jstar_scr[:, 0:1]


def _selected(key, col, thr, jstar):
    return jnp.where(key == thr, (col <= jstar).astype(jnp.int32), (key > thr).astype(jnp.int32))


def _select_kernel(iq_ref, iw_ref, ik2_ref, mask_ref, keys_scr, jstar_scr, gmax_scr, *, tq, tk, topk, seq):
    i = pl.program_id(1)
    nch = ((i + 1) * tq + tk - 1) // tk
    iqb = iq_ref[0].astype(BF16)
    w = iw_ref[0]
    row = i * tq + lax.broadcasted_iota(jnp.int32, (tq, 1), 0)
    lane = lax.broadcasted_iota(jnp.int32, (1, tk), 1)

    def score_chunk(c, carry):
        off = pl.multiple_of(c * tk, tk)
        k2 = ik2_ref[0, :, pl.ds(pl.multiple_of(c * 2 * tk, 2 * tk), 2 * tk)]
        acc = jnp.zeros((tq, tk), F32)
        for pr in range(IDX_HEADS // 2):
            r = jnp.dot(iqb[:, pr * 128:(pr + 1) * 128], k2, preferred_element_type=F32)
            acc = acc + w[:, 2 * pr:2 * pr + 1] * jnp.maximum(r[:, :tk], 0.0)
            acc = acc + w[:, 2 * pr + 1:2 * pr + 2] * jnp.maximum(r[:, tk:], 0.0)
        key = jnp.where(off + lane <= row, _sortable_key(acc), INT_MIN)
        keys_scr[:, pl.ds(off, tk)] = key
        for par in range(2):
            g = gmax_scr[:, par * LANES:(par + 1) * LANES]
            for j in range(par, tk // LANES, 2):
                g = jnp.maximum(g, key[:, j * LANES:(j + 1) * LANES])
            gmax_scr[:, par * LANES:(par + 1) * LANES] = g
        return carry

    assert tk % (2 * LANES) == 0 and topk <= 2 * LANES
    gmax_scr[...] = jnp.full(gmax_scr.shape, INT_MIN, jnp.int32)
    lax.fori_loop(0, nch, score_chunk, 0)
    lo, hi = _group_bounds(gmax_scr[...])

    rb = min(tq, 128)

    def count(pred, *ops):
        parts = []
        for r0 in range(0, tq, rb):
            ops_r = [jnp.broadcast_to(o[r0:r0 + rb], (rb, LANES)) for o in ops]

            def body(c, acc, r0=r0, ops_r=ops_r):
                for j in range(tk // LANES):
                    off = pl.multiple_of(c * tk + j * LANES, LANES)
                    hit = pred(keys_scr[r0:r0 + rb, pl.ds(off, LANES)], off + lane[:, 0:LANES], *ops_r)
                    acc = acc + hit.astype(jnp.int32)
                return acc

            acc = lax.fori_loop(0, nch, body, jnp.zeros((rb, LANES), jnp.int32))
            parts.append(jnp.sum(acc, axis=1, keepdims=True))
        return jnp.concatenate(parts, axis=0)

    thr, jstar = _topk_cut(count, topk, lo, hi, seq, jstar_scr)

    def emit(c, carry):
        off = pl.multiple_of(c * tk, tk)
        sel = _selected(keys_scr[:, pl.ds(off, tk)], off + lane, thr, jstar)
        mask_ref[0, :, pl.ds(off, tk)] = sel.astype(jnp.int8)
        return carry

    lax.fori_loop(0, nch, emit, 0)

    def clear(c, carry):
        mask_ref[0, :, pl.ds(pl.multiple_of(c * tk, tk), tk)] = jnp.zeros((tq, tk), jnp.int8)
        return carry

    lax.fori_loop(nch, seq // tk, clear, 0)


def _block_diag_keys(ik, tk):
    b, s, d = ik.shape
    kt = jnp.swapaxes(ik.astype(BF16), 1, 2).reshape(b, d, s // tk, tk)
    z = jnp.zeros_like(kt)
    top = jnp.stack([kt, z], axis=3)
    bot = jnp.stack([z, kt], axis=3)
    return jnp.concatenate([top, bot], axis=1).reshape(b, 2 * d, 2 * s)


def _select_prompt(iq, iw, ik, topk, tq, tk):
    b, s, _ = iq.shape
    ik2 = _block_diag_keys(ik, tk)
    return pl.pallas_call(
        functools.partial(_select_kernel, tq=tq, tk=tk, topk=topk, seq=s),
        grid=(b, s // tq),
        in_specs=[pl.BlockSpec((1, tq, IDX_HEADS * IDX_DIM), lambda bb, i: (bb, i, 0)),
                  pl.BlockSpec((1, tq, IDX_HEADS), lambda bb, i: (bb, i, 0)),
                  pl.BlockSpec((1, 2 * IDX_DIM, 2 * s), lambda bb, i: (bb, 0, 0))],
        out_specs=pl.BlockSpec((1, tq, s), lambda bb, i: (bb, i, 0)),
        out_shape=jax.ShapeDtypeStruct((b, s, s), jnp.int8),
        scratch_shapes=[pltpu.VMEM((tq, s), jnp.int32), pltpu.VMEM((tq, LANES), jnp.int32),
                        pltpu.VMEM((tq, 2 * LANES), jnp.int32)],
        compiler_params=_cparams(("arbitrary", "arbitrary")), name="select_prompt",
    )(iq, iw, ik2)


def _outnorm_kernel(x_ref, mf_ref, md_ref, w1_ref, w2_ref, g_ref, b_ref, o_ref):
    y = jnp.dot(mf_ref[...].astype(BF16), w1_ref[...], preferred_element_type=F32)
    y = y + jnp.dot(md_ref[...].astype(BF16), w2_ref[...], preferred_element_type=F32)
    z = DEEPNORM_ALPHA * x_ref[...] + y
    mu = jnp.mean(z, axis=1, keepdims=True)
    zc = z - mu
    var = jnp.mean(zc * zc, axis=1, keepdims=True)
    o_ref[...] = zc * lax.rsqrt(var + LN_EPS) * g_ref[...] + b_ref[...]


def _output_and_norm(x, mix_fox, mix_dsa, w_out, ln_g, ln_b, tm):
    n = x.shape[0]
    w1 = w_out[:WIDTH].astype(BF16)
    w2 = w_out[WIDTH:].astype(BF16)
    row = lambda i: (i, 0)
    const = lambda i: (0, 0)
    return pl.pallas_call(
        _outnorm_kernel, grid=(n // tm,),
        in_specs=[pl.BlockSpec((tm, D_MODEL), row), pl.BlockSpec((tm, WIDTH), row), pl.BlockSpec((tm, WIDTH), row),
                  pl.BlockSpec((WIDTH, D_MODEL), const), pl.BlockSpec((WIDTH, D_MODEL), const),
                  pl.BlockSpec((1, D_MODEL), const), pl.BlockSpec((1, D_MODEL), const)],
        out_specs=pl.BlockSpec((tm, D_MODEL), row),
        out_shape=jax.ShapeDtypeStruct((n, D_MODEL), F32),
        compiler_params=_cparams(("arbitrary",)), name="outnorm",
    )(x, mix_fox, mix_dsa, w1, w2, ln_g[None, :].astype(F32), ln_b[None, :].astype(F32))


def _prompt_tiles(s):
    tm = min(512, s)
    t_attn = min(512, s)
    tq_sel = min(256, s)
    tk_sel = min(512, s)
    return tm, t_attn, tq_sel, tk_sel


def _prompt_group(x, w_in, b_f, w_out, ln_g, ln_b):
    b, s, d = x.shape
    tm, t_attn, tq_sel, tk_sel = _prompt_tiles(s)
    x2 = x.reshape(b * s, d)
    fq, fk, fv, logf, fg, dq, dk, dv, dg, iq, ik, iw = _project(x2, jnp.arange(s), w_in, b_f, tm)
    r3 = lambda a: a.reshape(b, s, a.shape[-1])
    c_row = _cumsum_lanes(jnp.swapaxes(r3(logf), 1, 2).reshape(b * FOX_HEADS, s)).reshape(b, FOX_HEADS, s)
    c_tok = jnp.swapaxes(c_row, 1, 2)
    mix_fox = _flash("fox", r3(fq), r3(fk), r3(fv), r3(fg), (c_tok, c_row), t_attn)
    topk = min(TOPK_MAX, s // 4)
    mask = _select_prompt(r3(iq), r3(iw), r3(ik), topk, tq_sel, tk_sel)
    mix_dsa = _flash("dsa", r3(dq), r3(dk), r3(dv), r3(dg), mask, t_attn)
    y = _output_and_norm(x2, mix_fox.reshape(b * s, WIDTH), mix_dsa.reshape(b * s, WIDTH), w_out, ln_g, ln_b, tm)
    heads = lambda a, h: a.reshape(b, 1, s, h, HEAD_DIM)
    return (y.reshape(b, s, d), heads(fk, FOX_HEADS), heads(fv, FOX_HEADS), logf.reshape(b, 1, s, FOX_HEADS),
            heads(dk, DSA_HEADS), heads(dv, DSA_HEADS), ik.reshape(b, 1, s, IDX_DIM))


def _page_specs(block, npages_per_step, n_pages):
    zeros = (0,) * (len(block) - 1)
    return [pl.BlockSpec(block, lambda b, s, pt, g=g: (pt[b * n_pages + s * npages_per_step + g],) + zeros)
            for g in range(npages_per_step)]


def _gather_rows_kernel(pt_ref, *refs):
    pages, o_ref = refs[:-1], refs[-1]
    for g, page in enumerate(pages):
        o_ref[0, g:g + 1, :] = page[0]


def _gather_rows(cache_rows, pt_flat, db, n_pages, gp):
    w = cache_rows.shape[-1]
    grid_spec = pltpu.PrefetchScalarGridSpec(
        num_scalar_prefetch=1, grid=(db, n_pages // gp), in_specs=_page_specs((1, 1, w), gp, n_pages),
        out_specs=pl.BlockSpec((1, gp, w), lambda b, s, pt: (b, s, 0)))
    return pl.pallas_call(
        _gather_rows_kernel, grid_spec=grid_spec, out_shape=jax.ShapeDtypeStruct((db, n_pages, w), F32),
        compiler_params=_cparams(("arbitrary", "arbitrary")), name="gather_rows",
    )(pt_flat, *([cache_rows] * gp))


def _dot_exact(x, binary):
    hi = x.astype(BF16)
    r1 = x - hi.astype(F32)
    mid = r1.astype(BF16)
    lo = (r1 - mid.astype(F32)).astype(BF16)
    return (jnp.dot(hi, binary, preferred_element_type=F32) + jnp.dot(mid, binary, preferred_element_type=F32)
            + jnp.dot(lo, binary, preferred_element_type=F32))


def _past_decay_kernel(l_ref, within_ref, total_ref, o_ref):
    x = l_ref[0]
    n_pages = x.shape[0]
    within = _dot_exact(x, within_ref[...])
    totals = _dot_exact(x, total_ref[...])
    later = (lax.broadcasted_iota(jnp.int32, (n_pages, n_pages), 1)
             > lax.broadcasted_iota(jnp.int32, (n_pages, n_pages), 0)).astype(BF16)
    hi = totals.astype(BF16)
    r1 = totals - hi.astype(F32)
    mid = r1.astype(BF16)
    lo = (r1 - mid.astype(F32)).astype(BF16)
    across = (jnp.dot(later, hi, preferred_element_type=F32) + jnp.dot(later, mid, preferred_element_type=F32)
              + jnp.dot(later, lo, preferred_element_type=F32))
    o_ref[0] = within + across


def _past_decay(past_logf):
    db, n_pages, w = past_logf.shape
    i = jnp.arange(w)
    same_head = (i[:, None] % FOX_HEADS) == (i[None, :] % FOX_HEADS)
    within = jnp.logical_and(same_head, i[:, None] > i[None, :]).astype(BF16)
    total = same_head.astype(BF16)
    const = lambda b: (0, 0)
    per_b = pl.BlockSpec((1, n_pages, w), lambda b: (b, 0, 0))
    return pl.pallas_call(
        _past_decay_kernel, grid=(db,),
        in_specs=[per_b, pl.BlockSpec((w, w), const), pl.BlockSpec((w, w), const)], out_specs=per_b,
        out_shape=jax.ShapeDtypeStruct((db, n_pages, w), F32),
        compiler_params=_cparams(("arbitrary",)), name="past_decay",
    )(past_logf, within, total)


def _select_sample_kernel(pt_ref, iqm_ref, wcol_ref, iknew_ref, *refs, gs, topk, past, T):
    pages = refs[:gs]
    mp_ref, mn_ref, keys_scr, jstar_scr = refs[gs:]
    s_idx = pl.program_id(1)
    iqm = iqm_ref[0]
    wcol = wcol_ref[0]
    ncols = past + PAGE_SIZE

    def scores(kblk):
        r = lax.dot_general(iqm, kblk.astype(BF16), (((1,), (1,)), ((), ())), preferred_element_type=F32)
        r = wcol * jnp.maximum(r, 0.0)
        acc = r[0:T]
        for h in range(1, IDX_HEADS):
            acc = acc + r[h * T:(h + 1) * T]
        return acc

    for g in range(gs):
        off = pl.multiple_of((s_idx * gs + g) * PAGE_SIZE, PAGE_SIZE)
        keys_scr[:, pl.ds(off, PAGE_SIZE)] = _sortable_key(scores(pages[g][0]))

    @pl.when(s_idx == pl.num_programs(1) - 1)
    def _():
        t_col = lax.broadcasted_iota(jnp.int32, (T, 1), 0)
        j_row = lax.broadcasted_iota(jnp.int32, (1, PAGE_SIZE), 1)
        keys_scr[:, past:ncols] = jnp.where(j_row <= t_col, _sortable_key(scores(iknew_ref[0])), INT_MIN)
        col = lax.broadcasted_iota(jnp.int32, (1, ncols), 1)

        def count(pred, *ops):
            return jnp.sum(pred(keys_scr[...], col, *ops).astype(jnp.int32), axis=1, keepdims=True)

        assert topk <= 2 * LANES
        gmax = []
        for par in range(2):
            g = jnp.full((T, LANES), INT_MIN, jnp.int32)
            for j in range(par, ncols // LANES, 2):
                g = jnp.maximum(g, keys_scr[:, j * LANES:(j + 1) * LANES])
            gmax.append(g)
        lo, hi = _group_bounds(jnp.concatenate(gmax, axis=1))
        thr, jstar = _topk_cut(count, topk, lo, hi, ncols, jstar_scr)
        sel = _selected(keys_scr[...], col, thr, jstar).astype(F32)
        for pg in range(past // PAGE_SIZE):
            mp_ref[0, pg] = sel[:, pg * PAGE_SIZE:(pg + 1) * PAGE_SIZE]
        mn_ref[0] = sel[:, past:]


def _select_sample(iq, iw, ik_new, cache_ik, pt_flat, n_pages, topk, gs):
    db, T, _ = iq.shape
    past = n_pages * PAGE_SIZE
    rows = IDX_HEADS * T
    iqm = jnp.swapaxes(iq.reshape(db, T, IDX_HEADS, IDX_DIM), 1, 2).reshape(db, rows, IDX_DIM).astype(BF16)
    wcol = jnp.swapaxes(iw, 1, 2).reshape(db, rows, 1)
    ik_pad = jnp.pad(ik_new, ((0, 0), (0, PAGE_SIZE - T), (0, 0)))
    per_b = lambda blk: pl.BlockSpec(blk, lambda b, s, pt: (b, 0, 0))
    grid_spec = pltpu.PrefetchScalarGridSpec(
        num_scalar_prefetch=1, grid=(db, n_pages // gs),
        in_specs=[per_b((1, rows, IDX_DIM)), per_b((1, rows, 1)), per_b((1, PAGE_SIZE, IDX_DIM))]
        + _page_specs((1, PAGE_SIZE, IDX_DIM), gs, n_pages),
        out_specs=(pl.BlockSpec((1, n_pages, T, PAGE_SIZE), lambda b, s, pt: (b, 0, 0, 0)), per_b((1, T, PAGE_SIZE))),
        scratch_shapes=[pltpu.VMEM((T, past + PAGE_SIZE), jnp.int32), pltpu.VMEM((T, LANES), jnp.int32)])
    return pl.pallas_call(
        functools.partial(_select_sample_kernel, gs=gs, topk=topk, past=past, T=T), grid_spec=grid_spec,
        out_shape=(jax.ShapeDtypeStruct((db, n_pages, T, PAGE_SIZE), F32),
                   jax.ShapeDtypeStruct((db, T, PAGE_SIZE), F32)),
        compiler_params=_cparams(("arbitrary", "arbitrary")), name="select_sample",
    )(pt_flat, iqm, wcol, ik_pad, *([cache_ik] * gs))


def _decode_kernel(pt_ref, q_ref, knew_ref, vnew_ref, gate_ref, *refs, mode, G, nheads, T):
    if mode == "fox":
        pd_ref, lfcol_ref, lfrow_ref = refs[:3]
        refs = refs[3:]
    else:
        mp_ref, mn_ref = refs[:2]
        refs = refs[2:]
    k_refs, v_refs = refs[:G], refs[G:2 * G]
    o_ref, m_scr, l_scr, acc_scr = refs[2 * G:2 * G + 4]
    s_idx = pl.program_id(1)
    rows = nheads * T
    vkeys = PAGE_SIZE * nheads
    nt = (((1,), (1,)), ((), ()))

    @pl.when(s_idx == 0)
    def _():
        m_scr[...] = jnp.full(m_scr.shape, MASK_NEG, F32)
        l_scr[...] = jnp.zeros(l_scr.shape, F32)
        acc_scr[...] = jnp.zeros(acc_scr.shape, F32)

    q = q_ref[0]
    row_head = lax.broadcasted_iota(jnp.int32, (rows, 1), 0) // T
    row_tok = lax.broadcasted_iota(jnp.int32, (rows, 1), 0) % T
    lane = lax.broadcasted_iota(jnp.int32, (1, vkeys), 1)
    own = jnp.where((lane % nheads) == row_head, 0.0, MASK_NEG)
    if mode == "fox":
        ri = lax.broadcasted_iota(jnp.int32, (rows, rows), 0)
        ci = lax.broadcasted_iota(jnp.int32, (rows, rows), 1)
        tri_col = jnp.logical_and(ri // T == ci // T, ci % T <= ri % T).astype(F32)
        nc_col = jnp.dot(tri_col, lfcol_ref[0], precision=lax.Precision.HIGHEST, preferred_element_type=F32)
        ui = lax.broadcasted_iota(jnp.int32, (LANES, LANES), 0)
        uj = lax.broadcasted_iota(jnp.int32, (LANES, LANES), 1)
        tri_row = jnp.logical_and(jnp.logical_and(ui % nheads == uj % nheads, ui <= uj), uj < rows).astype(F32)
        nc_row = jnp.dot(lfrow_ref[0], tri_row, precision=lax.Precision.HIGHEST, preferred_element_type=F32)
        own_nc = own + nc_col[:, 0:1]
    else:
        expand = (lax.broadcasted_iota(jnp.int32, (PAGE_SIZE, vkeys), 1) // nheads
                  == lax.broadcasted_iota(jnp.int32, (PAGE_SIZE, vkeys), 0)).astype(BF16)
        picked = jnp.dot(mp_ref[0].reshape(G * T, PAGE_SIZE).astype(BF16), expand, preferred_element_type=F32)

    def update(s, vals):
        width = vals[0].shape[0]
        m_prev = m_scr[:, 0:1]
        m_new = jnp.maximum(m_prev, jnp.max(s, axis=1, keepdims=True))
        p = jnp.exp(s - m_new)
        alpha = jnp.exp(m_prev - m_new)
        l_scr[...] = jnp.broadcast_to(alpha * l_scr[:, 0:1] + jnp.sum(p, axis=1, keepdims=True), l_scr.shape)
        p = p.astype(BF16)
        pv = jnp.dot(p[:, 0:width], vals[0], preferred_element_type=F32)
        for g in range(1, len(vals)):
            pv = pv + jnp.dot(p[:, g * width:(g + 1) * width], vals[g], preferred_element_type=F32)
        acc_scr[...] = alpha * acc_scr[...] + pv
        m_scr[...] = jnp.broadcast_to(m_new, m_scr.shape)

    parts = []
    for g in range(G):
        s = lax.dot_general(q, k_refs[g][0].astype(BF16), nt, preferred_element_type=F32) * ATTN_SCALE
        if mode == "fox":
            s = s + (own_nc + pd_ref[0, g:g + 1, :])
        else:
            keep = jnp.concatenate([picked[g * T:(g + 1) * T]] * nheads, axis=0) > 0.5
            s = jnp.where(keep, s + own, MASK_NEG)
        parts.append(s)
    update(jnp.concatenate(parts, axis=1), [v[0].astype(BF16) for v in v_refs])

    @pl.when(s_idx == pl.num_programs(1) - 1)
    def _():
        sn = lax.dot_general(q, knew_ref[0].astype(BF16), nt, preferred_element_type=F32) * ATTN_SCALE
        lane_n = lax.broadcasted_iota(jnp.int32, (1, LANES), 1)
        if mode == "fox":
            keep = (lane_n // nheads) <= row_tok
            sn = sn + (nc_col[:, 0:1] - nc_row[0:1, :])
        else:
            keep = jnp.concatenate([jnp.dot(mn_ref[0].astype(BF16), expand[:, 0:LANES],
                                            preferred_element_type=F32)] * nheads, axis=0) > 0.5
        sn = jnp.where(keep, sn + own[:, 0:LANES], MASK_NEG)
        update(sn, [vnew_ref[0].astype(BF16)])
        o = acc_scr[...] / l_scr[:, 0:1]
        for h in range(nheads):
            sl = slice(h * HEAD_DIM, (h + 1) * HEAD_DIM)
            g = gate_ref[0, :, sl]
            o_ref[0, :, sl] = o[h * T:(h + 1) * T] * (g / (1.0 + jnp.exp(-g)))


def _decode(mode, q, k_new, v_new, gate, extra, cache_k, cache_v, pt_flat, n_pages, G):
    db, T, w = q.shape
    nheads = w // HEAD_DIM
    rows = nheads * T
    assert rows <= LANES and nheads * HEAD_DIM == w
    vkeys = PAGE_SIZE * nheads
    q_rows = jnp.swapaxes(q.reshape(db, T, nheads, HEAD_DIM), 1, 2).reshape(db, rows, HEAD_DIM).astype(BF16)
    new_rows = lambda a: jnp.pad(a.reshape(db, rows, HEAD_DIM), ((0, 0), (0, LANES - rows), (0, 0)))
    per_b = lambda blk: pl.BlockSpec(blk, lambda b, s, pt: (b,) + (0,) * (len(blk) - 1))
    step = lambda blk: pl.BlockSpec(blk, lambda b, s, pt: (b, s) + (0,) * (len(blk) - 2))
    in_specs = [per_b((1, rows, HEAD_DIM)), per_b((1, LANES, HEAD_DIM)), per_b((1, LANES, HEAD_DIM)),
                per_b((1, T, w))]
    if mode == "fox":
        pd, logf_new = extra
        lf_col = jnp.broadcast_to(jnp.swapaxes(logf_new, 1, 2).reshape(db, rows, 1), (db, rows, LANES))
        lf_row = jnp.broadcast_to(jnp.pad(logf_new.reshape(db, 1, rows), ((0, 0), (0, 0), (0, LANES - rows))),
                                  (db, 8, LANES))
        in_specs += [step((1, G, vkeys)), per_b((1, rows, LANES)), per_b((1, 8, LANES))]
        args = (pd, lf_col, lf_row)
    else:
        mask_past, mask_new = extra
        in_specs += [step((1, G, T, PAGE_SIZE)), per_b((1, T, PAGE_SIZE))]
        args = (mask_past, mask_new)
    in_specs += _page_specs((1, vkeys, HEAD_DIM), G, n_pages) + _page_specs((1, vkeys, HEAD_DIM), G, n_pages)
    grid_spec = pltpu.PrefetchScalarGridSpec(
        num_scalar_prefetch=1, grid=(db, n_pages // G), in_specs=in_specs, out_specs=per_b((1, T, w)),
        scratch_shapes=[pltpu.VMEM((rows, LANES), F32), pltpu.VMEM((rows, LANES), F32),
                        pltpu.VMEM((rows, HEAD_DIM), F32)])
    return pl.pallas_call(
        functools.partial(_decode_kernel, mode=mode, G=G, nheads=nheads, T=T), grid_spec=grid_spec,
        out_shape=jax.ShapeDtypeStruct((db, T, w), F32),
        compiler_params=_cparams(("arbitrary", "arbitrary")), name=f"decode_{mode}",
    )(pt_flat, q_rows, new_rows(k_new), new_rows(v_new), gate, *args, *([cache_k] * G), *([cache_v] * G))


def _sample_group(x, caches, page_table, w_in, b_f, w_out, ln_g, ln_b):
    cache_fox_k, cache_fox_v, cache_fox_logf, cache_dsa_k, cache_dsa_v, cache_idx_k = caches
    db, T, d = x.shape
    n_pages = page_table.shape[1]
    past = n_pages * PAGE_SIZE
    nphys = cache_fox_k.shape[0]
    G = min(8, n_pages)
    pt_flat = page_table.reshape(-1).astype(jnp.int32)
    x2 = x.reshape(db * T, d)
    pos = jnp.tile(past + jnp.arange(T), db)
    fq, fk, fv, logf, fg, dq, dk, dv, dg, iq, ik, iw = _project(x2, pos, w_in, b_f, db * T)
    r3 = lambda a: a.reshape(db, T, a.shape[-1])
    kv_rows = lambda c: c.reshape(nphys, PAGE_SIZE * c.shape[3], HEAD_DIM)
    past_logf = _gather_rows(cache_fox_logf.reshape(nphys, 1, PAGE_SIZE * FOX_HEADS), pt_flat, db, n_pages,
                             min(32, n_pages))
    pd = _past_decay(past_logf)
    mix_fox = _decode("fox", r3(fq), r3(fk), r3(fv), r3(fg), (pd, r3(logf)),
                      kv_rows(cache_fox_k), kv_rows(cache_fox_v), pt_flat, n_pages, G)
    topk = min(TOPK_MAX, (past + T) // 4)
    masks = _select_sample(r3(iq), r3(iw), r3(ik), cache_idx_k.reshape(nphys, PAGE_SIZE, IDX_DIM), pt_flat, n_pages,
                           topk, min(16, n_pages))
    mix_dsa = _decode("dsa", r3(dq), r3(dk), r3(dv), r3(dg), masks,
                      kv_rows(cache_dsa_k), kv_rows(cache_dsa_v), pt_flat, n_pages, G)
    y = _output_and_norm(x2, mix_fox.reshape(db * T, WIDTH), mix_dsa.reshape(db * T, WIDTH), w_out, ln_g, ln_b,
                         db * T)
    heads = lambda a, h: a.reshape(db, 1, T, h, HEAD_DIM)
    return (y.reshape(db, T, d), heads(fk, FOX_HEADS), heads(fv, FOX_HEADS), logf.reshape(db, 1, T, FOX_HEADS),
            heads(dk, DSA_HEADS), heads(dv, DSA_HEADS), ik.reshape(db, 1, T, IDX_DIM))


def kernel(x_prompt, x_sample, cache_fox_k, cache_fox_v, cache_fox_logf, cache_dsa_k, cache_dsa_v, cache_idx_k,
           page_table, w_in, b_f, w_out, ln_g, ln_b):
    assert w_in.shape[0] == DEPTH and cache_fox_k.shape[1] == DEPTH
    params = (w_in[0], b_f[0], w_out[0], ln_g[0], ln_b[0])
    caches = (cache_fox_k, cache_fox_v, cache_fox_logf, cache_dsa_k, cache_dsa_v, cache_idx_k)
    p = _prompt_group(x_prompt, *params)
    s = _sample_group(x_sample, caches, page_table, *params)
    return (p[0], s[0], *p[1:], *s[1:])
```

```python
import functools
import math

import jax
import jax.numpy as jnp
import numpy as np
from jax import lax
from jax.experimental import pallas as pl
from jax.experimental.pallas import tpu as pltpu

F32 = jnp.float32
BF16 = jnp.bfloat16

D_MODEL = 2048
HEAD_DIM = 128
FOX_HEADS = 8
DSA_HEADS = 8
WIDTH = FOX_HEADS * HEAD_DIM
IDX_HEADS = 16
IDX_DIM = 64
TOPK_MAX = 256
PAGE_SIZE = 128
ROPE_THETA = 10000.0
LN_EPS = 1e-5
ATTN_SCALE = HEAD_DIM ** -0.5
LOG2E = math.log2(math.e)
IDX_W_SCALE = IDX_HEADS ** -0.5 * IDX_DIM ** -0.5
DEPTH = 1
DEEPNORM_ALPHA = (2.0 * DEPTH) ** 0.25
WIDTHS = (WIDTH, WIDTH, WIDTH, FOX_HEADS, WIDTH, WIDTH, WIDTH, WIDTH, WIDTH,
          IDX_HEADS * IDX_DIM, IDX_DIM, IDX_HEADS)
OFFS = tuple(int(v) for v in np.cumsum((0,) + WIDTHS))

LANES = 128
VMEM_LIMIT_BYTES = 56 * 1024 * 1024
MASK_NEG = -1e30
SOFTMAX_ROWS = 64
HEAD_GROUP = 2
INT_MIN = -(2 ** 31)


def _cparams(sem):
    return pltpu.CompilerParams(dimension_semantics=sem, vmem_limit_bytes=VMEM_LIMIT_BYTES)


def _rope_tables(pos, dim):
    half = dim // 2
    inv = ROPE_THETA ** (-jnp.arange(half, dtype=F32) * 2.0 / dim)
    ang = pos.astype(F32)[:, None] * inv[None, :]
    cos, sin = jnp.cos(ang), jnp.sin(ang)
    zero = jnp.zeros_like(sin)
    if dim == LANES:
        return (jnp.concatenate([cos, cos], 1), jnp.concatenate([-sin, sin], 1))
    return (jnp.concatenate([cos, cos, cos, cos], 1),
            jnp.concatenate([-sin, zero, -sin, zero], 1),
            jnp.concatenate([zero, sin, zero, sin], 1))


def _rope128(blk, cos, sin):
    return blk * cos + pltpu.roll(blk, 64, axis=1) * sin


def _rope64(blk, cos, sin_lo, sin_hi):
    return blk * cos + pltpu.roll(blk, 96, axis=1) * sin_lo + pltpu.roll(blk, 32, axis=1) * sin_hi


def _proj_kernel(x_ref, *refs, kind, nseg):
    xb = x_ref[...]
    w_refs, refs = refs[:nseg], refs[nseg:]
    if kind == "plain":
        for w_ref, o_ref in zip(w_refs, refs):
            for c in range(0, WIDTH, 256):
                o_ref[:, c:c + 256] = jnp.dot(xb, w_ref[:, c:c + 256], preferred_element_type=F32)
    elif kind == "rope128":
        cos, sin = refs[0][...], refs[1][...]
        for w_ref, o_ref in zip(w_refs, refs[2:]):
            for c in range(0, WIDTH, 256):
                acc = jnp.dot(xb, w_ref[:, c:c + 256], preferred_element_type=F32)
                for h in range(2):
                    o_ref[:, c + h * 128:c + (h + 1) * 128] = _rope128(acc[:, h * 128:(h + 1) * 128], cos, sin)
    elif kind == "rope64":
        cos, slo, shi = refs[0][...], refs[1][...], refs[2][...]
        for w_ref, o_ref in zip(w_refs, refs[3:]):
            for c in range(0, WIDTH, 256):
                acc = jnp.dot(xb, w_ref[:, c:c + 256], preferred_element_type=F32)
                for h in range(2):
                    o_ref[:, c + h * 128:c + (h + 1) * 128] = _rope64(acc[:, h * 128:(h + 1) * 128], cos, slo, shi)
    else:
        (w_ref,) = w_refs
        cos_ref, slo_ref, shi_ref, bf_ref, ik_ref, logf_ref, iw_ref = refs
        acc = jnp.dot(xb, w_ref[...], preferred_element_type=F32)
        ik = _rope64(acc[:, 0:128], cos_ref[...], slo_ref[...], shi_ref[...])
        ik_ref[...] = ik[:, :IDX_DIM]
        z = acc[:, 128:256] + bf_ref[...]
        logf = jnp.minimum(z, 0.0) - jnp.log(1.0 + jnp.exp(-jnp.abs(z)))
        logf_ref[...] = logf[:, :FOX_HEADS]
        iw_ref[...] = (acc[:, 256:384] * IDX_W_SCALE)[:, :IDX_HEADS]


def _project_segments(x, ws, kind, tables, tm, extra=None):
    n = x.shape[0]
    grid = (n // tm,)
    tab_blocks = tables[0].shape[0] // tm if tables else 1
    row = lambda i: (i, 0)
    tab = lambda i: (i % tab_blocks, 0)
    in_specs = [pl.BlockSpec((tm, D_MODEL), row)]
    in_specs += [pl.BlockSpec((D_MODEL, w.shape[1]), lambda i: (0, 0)) for w in ws]
    in_specs += [pl.BlockSpec((tm, LANES), tab) for _ in tables]
    args = [x, *ws, *tables]
    if kind == "small":
        in_specs.append(pl.BlockSpec((1, LANES), lambda i: (0, 0)))
        args.append(extra)
        out_shape = (jax.ShapeDtypeStruct((n, IDX_DIM), F32), jax.ShapeDtypeStruct((n, FOX_HEADS), F32),
                     jax.ShapeDtypeStruct((n, IDX_HEADS), F32))
        out_specs = (pl.BlockSpec((tm, IDX_DIM), row), pl.BlockSpec((tm, FOX_HEADS), row),
                     pl.BlockSpec((tm, IDX_HEADS), row))
    else:
        out_shape = tuple(jax.ShapeDtypeStruct((n, w.shape[1]), F32) for w in ws)
        out_specs = tuple(pl.BlockSpec((tm, w.shape[1]), row) for w in ws)
    return pl.pallas_call(
        functools.partial(_proj_kernel, kind=kind, nseg=len(ws)),
        grid=grid, in_specs=in_specs, out_specs=out_specs, out_shape=out_shape,
        compiler_params=_cparams(("arbitrary",)), name=f"proj_{kind}",
    )(*args)


def _project(x, pos_rows, w_in, b_f, tm):
    seg = lambda i: w_in[:, OFFS[i]:OFFS[i + 1]].astype(BF16)
    xb = x.astype(BF16)
    t128 = _rope_tables(pos_rows, HEAD_DIM)
    t64 = _rope_tables(pos_rows, IDX_DIM)
    fq, fk = _project_segments(xb, [seg(0), seg(1)], "plain", (), tm)
    fv, fg = _project_segments(xb, [seg(2), seg(4)], "plain", (), tm)
    dq, dk = _project_segments(xb, [seg(5), seg(6)], "rope128", t128, tm)
    dv, dg = _project_segments(xb, [seg(7), seg(8)], "plain", (), tm)
    (iq,) = _project_segments(xb, [seg(9)], "rope64", t64, tm)
    pad = lambda a: jnp.pad(a, ((0, 0), (0, LANES - a.shape[1])))
    w_small = jnp.concatenate([pad(w_in[:, OFFS[10]:OFFS[11]]), pad(w_in[:, OFFS[3]:OFFS[4]]),
                               pad(w_in[:, OFFS[11]:OFFS[12]])], axis=1).astype(BF16)
    ik, logf, iw = _project_segments(xb, [w_small], "small", t64, tm, extra=pad(b_f[None, :].astype(F32)))
    return fq, fk, fv, logf, fg, dq, dk, dv, dg, iq, ik, iw


def _cumsum_kernel(x_ref, o_ref, *, suffix):
    rows, n = x_ref.shape
    r = lax.broadcasted_iota(jnp.int32, (LANES, LANES), 0)
    c = lax.broadcasted_iota(jnp.int32, (LANES, LANES), 1)
    tri = ((r > c) if suffix else (r <= c)).astype(F32)
    carry = jnp.zeros((rows, 1), F32)
    nchunks = n // LANES
    for step in range(nchunks):
        j = nchunks - 1 - step if suffix else step
        x = x_ref[:, j * LANES:(j + 1) * LANES]
        blk = jnp.dot(x, tri, precision=lax.Precision.HIGHEST, preferred_element_type=F32) + carry
        o_ref[:, j * LANES:(j + 1) * LANES] = blk
        carry = (blk[:, 0:1] + x[:, 0:1]) if suffix else blk[:, LANES - 1:LANES]


def _cumsum_lanes(x, suffix=False):
    rows, n = x.shape
    tr = min(rows, 64)
    return pl.pallas_call(
        functools.partial(_cumsum_kernel, suffix=suffix), grid=(rows // tr,),
        in_specs=[pl.BlockSpec((tr, n), lambda i: (i, 0))], out_specs=pl.BlockSpec((tr, n), lambda i: (i, 0)),
        out_shape=jax.ShapeDtypeStruct(x.shape, F32), compiler_params=_cparams(("arbitrary",)),
        name="cumsum_suffix" if suffix else "cumsum")(x)


def _flash_kernel(qi_tab, ki_tab, q_ref, k_ref, v_ref, g_ref, *refs, mode, T, nheads):
    if mode == "fox":
        ctok_ref, crow_ref, o_ref, m_scr, acc_scr, bias_scr = refs
    else:
        mask_ref, o_ref, m_scr, acc_scr, bias_scr = refs
    t = pl.program_id(1)
    qi, ki = qi_tab[t], ki_tab[t]

    @pl.when(ki == 0)
    def _():
        m_scr[...] = jnp.full(m_scr.shape, MASK_NEG, F32)
        acc_scr[...] = jnp.zeros(acc_scr.shape, F32)

    nt = (((1,), (1,)), ((), ()))
    subs = [slice(r, r + SOFTMAX_ROWS) for r in range(0, T, SOFTMAX_ROWS)]

    def step(diag):
        if mode == "dsa":
            for rs in subs:
                bias_scr[rs] = jnp.where(mask_ref[0, rs].astype(jnp.int32) != 0, 0.0, MASK_NEG)
        elif diag:
            for rs in subs:
                col = lax.broadcasted_iota(jnp.int32, (SOFTMAX_ROWS, T), 1)
                row = rs.start + lax.broadcasted_iota(jnp.int32, (SOFTMAX_ROWS, T), 0)
                bias_scr[rs] = jnp.where(col <= row, 0.0, MASK_NEG)
        biased = mode == "dsa" or diag

        for h0 in range(0, nheads, HEAD_GROUP):
            heads = range(h0, min(h0 + HEAD_GROUP, nheads))
            cols = {h: slice(h * HEAD_DIM, (h + 1) * HEAD_DIM) for h in heads}
            s = {h: lax.dot_general(q_ref[0, :, cols[h]].astype(BF16), k_ref[0, :, cols[h]].astype(BF16), nt,
                                    preferred_element_type=F32) * (ATTN_SCALE * LOG2E) for h in heads}
            if mode == "fox":
                s = {h: s[h] + (ctok_ref[0, :, h:h + 1] * LOG2E - crow_ref[0, h:h + 1, :] * LOG2E) for h in heads}
            if biased:
                s = {h: s[h] + bias_scr[...] for h in heads}
            m_prev = {h: m_scr[h] for h in heads}
            m_new = {h: jnp.maximum(m_prev[h], jnp.broadcast_to(jnp.max(s[h], axis=1, keepdims=True), (T, LANES)))
                     for h in heads}
            p = {h: jnp.exp2(s[h] - jnp.concatenate([m_new[h]] * (T // LANES), axis=1)).astype(BF16) for h in heads}
            alpha = {h: jnp.exp2(m_prev[h] - m_new[h]) for h in heads}
            ones = jnp.ones((T, HEAD_DIM), BF16)
            pv = {h: jnp.dot(p[h], jnp.concatenate([v_ref[0, :, cols[h]].astype(BF16), ones], axis=1),
                             preferred_element_type=F32) for h in heads}
            for h in heads:
                acc_scr[h] = jnp.concatenate([alpha[h]] * 2, axis=1) * acc_scr[h] + pv[h]
                m_scr[h] = m_new[h]

    if mode == "dsa":
        step(False)
    else:
        pl.when(ki == qi)(functools.partial(step, True))
        pl.when(ki != qi)(functools.partial(step, False))

    @pl.when(ki == qi)
    def _():
        for h in range(nheads):
            sl = slice(h * HEAD_DIM, (h + 1) * HEAD_DIM)
            g = g_ref[0, :, sl]
            o = acc_scr[h, :, 0:HEAD_DIM] / acc_scr[h, :, HEAD_DIM:2 * HEAD_DIM]
            o_ref[0, :, sl] = (o * (g / (1.0 + jnp.exp(-g)))).astype(BF16)


def _flash(mode, q, k, v, gate, extra, T):
    b, s, w = q.shape
    nheads = w // HEAD_DIM
    nb = s // T
    pairs = [(i, j) for i in range(nb) for j in range(i + 1)]
    qi_tab = jnp.asarray([p[0] for p in pairs], jnp.int32)
    ki_tab = jnp.asarray([p[1] for p in pairs], jnp.int32)
    qmap = lambda bb, t, qt, kt: (bb, qt[t], 0)
    kmap = lambda bb, t, qt, kt: (bb, kt[t], 0)
    in_specs = [pl.BlockSpec((1, T, w), qmap), pl.BlockSpec((1, T, w), kmap), pl.BlockSpec((1, T, w), kmap),
                pl.BlockSpec((1, T, w), qmap)]
    if mode == "fox":
        c_tok, c_row = extra
        in_specs += [pl.BlockSpec((1, T, nheads), qmap),
                     pl.BlockSpec((1, nheads, T), lambda bb, t, qt, kt: (bb, 0, kt[t]))]
        args = (c_tok, c_row)
    else:
        in_specs += [pl.BlockSpec((1, T, T), lambda bb, t, qt, kt: (bb, qt[t], kt[t]))]
        args = (extra,)
    grid_spec = pltpu.PrefetchScalarGridSpec(
        num_scalar_prefetch=2, grid=(b, len(pairs)), in_specs=in_specs,
        out_specs=pl.BlockSpec((1, T, w), qmap),
        scratch_shapes=[pltpu.VMEM((nheads, T, LANES), F32), pltpu.VMEM((nheads, T, 2 * HEAD_DIM), F32),
                        pltpu.VMEM((T, T), F32)])
    return pl.pallas_call(
        functools.partial(_flash_kernel, mode=mode, T=T, nheads=nheads),
        grid_spec=grid_spec, out_shape=jax.ShapeDtypeStruct((b, s, w), BF16),
        compiler_params=_cparams(("arbitrary", "arbitrary")), name=f"flash_{mode}",
    )(qi_tab, ki_tab, q, k, v, gate, *args)


def _sortable_key(x):
    bits = pltpu.bitcast(x, jnp.int32)
    return bits ^ ((bits >> 31) & 0x7FFFFFFF)


def _group_bounds(group_max):
    return jnp.min(group_max, axis=1, keepdims=True), jnp.max(group_max, axis=1, keepdims=True)


def _kth_threshold(count_ge, topk, lo, hi):
    def cond(carry):
        return carry[-1] > 0

    def body(carry):
        lo, hi, exact, _ = carry
        mid = (lo | hi) - ((lo ^ hi) >> 1)
        n = count_ge(mid)
        live = jnp.logical_and(exact == 0, lo < hi)
        up = jnp.logical_and(live, n >= topk)
        lo = jnp.where(up, mid, lo)
        hi = jnp.where(jnp.logical_and(live, n < topk), mid - 1, hi)
        exact = jnp.where(jnp.logical_and(up, n == topk), 1, exact)
        return lo, hi, exact, jnp.sum(jnp.logical_and(exact == 0, lo < hi).astype(jnp.int32))

    e0 = jnp.zeros(lo.shape, jnp.int32)
    lo, _, exact, _ = lax.while_loop(cond, body, (lo, hi, e0, jnp.sum((lo < hi).astype(jnp.int32))))
    return lo, jnp.where(exact == 1, topk, -1)


def _topk_cut(count, topk, lo, hi, ncols, jstar_scr):
    rows = lo.shape[0]
    thr, n_ge_known = _kth_threshold(lambda cand: count(lambda key, col, c: key >= c, cand), topk, lo, hi)
    real = thr > INT_MIN
    jstar_scr[...] = jnp.broadcast_to(jnp.where(real, ncols - 1, -1).astype(jnp.int32), jstar_scr.shape)

    @pl.when(jnp.min(n_ge_known) < 0)
    def _():
        n_ge = count(lambda key, col, t: key >= t, thr)
        tie = jnp.logical_and(real, n_ge > topk)

        @pl.when(jnp.max(tie.astype(jnp.int32)) > 0)
        def _():
            need = topk - count(lambda key, col, t: key > t, thr)
            nbits = max(1, (ncols - 1).bit_length())

            def body(b, j):
                cand = j | (jnp.int32(1) << (nbits - 1 - b))
                below = count(lambda key, col, t, c: jnp.logical_and(key == t, col < c), thr, cand)
                return jnp.where(below < need, cand, j)

            j = lax.fori_loop(0, nbits, body, jnp.zeros((rows, 1), jnp.int32))
            jstar_scr[...] = jnp.broadcast_to(jnp.where(tie, j, jstar_scr[:, 0:1]), jstar_scr.shape)

    return thr, jstar_scr[:, 0:1]


def _selected(key, col, thr, jstar):
    return jnp.where(key == thr, (col <= jstar).astype(jnp.int32), (key > thr).astype(jnp.int32))


def _select_kernel(iq_ref, iw_ref, ik2_ref, mask_ref, keys_scr, jstar_scr, gmax_scr, *, tq, tk, topk, seq):
    i = pl.program_id(1)
    nch = ((i + 1) * tq + tk - 1) // tk
    iqb = iq_ref[0].astype(BF16)
    w = iw_ref[0]
    row = i * tq + lax.broadcasted_iota(jnp.int32, (tq, 1), 0)
    lane = lax.broadcasted_iota(jnp.int32, (1, tk), 1)

    def score_chunk(c, carry):
        off = pl.multiple_of(c * tk, tk)
        k2 = ik2_ref[0, :, pl.ds(pl.multiple_of(c * 2 * tk, 2 * tk), 2 * tk)]
        acc = jnp.zeros((tq, tk), F32)
        for pr in range(IDX_HEADS // 2):
            r = jnp.dot(iqb[:, pr * 128:(pr + 1) * 128], k2, preferred_element_type=F32)
            acc = acc + w[:, 2 * pr:2 * pr + 1] * jnp.maximum(r[:, :tk], 0.0)
            acc = acc + w[:, 2 * pr + 1:2 * pr + 2] * jnp.maximum(r[:, tk:], 0.0)
        key = jnp.where(off + lane <= row, _sortable_key(acc), INT_MIN)
        keys_scr[:, pl.ds(off, tk)] = key
        for par in range(2):
            g = gmax_scr[:, par * LANES:(par + 1) * LANES]
            for j in range(par, tk // LANES, 2):
                g = jnp.maximum(g, key[:, j * LANES:(j + 1) * LANES])
            gmax_scr[:, par * LANES:(par + 1) * LANES] = g
        return carry

    assert tk % (2 * LANES) == 0 and topk <= 2 * LANES
    gmax_scr[...] = jnp.full(gmax_scr.shape, INT_MIN, jnp.int32)
    lax.fori_loop(0, nch, score_chunk, 0)
    lo, hi = _group_bounds(gmax_scr[...])

    rb = min(tq, 128)

    def count(pred, *ops):
        parts = []
        for r0 in range(0, tq, rb):
            ops_r = [jnp.broadcast_to(o[r0:r0 + rb], (rb, LANES)) for o in ops]

            def body(c, acc, r0=r0, ops_r=ops_r):
                for j in range(tk // LANES):
                    off = pl.multiple_of(c * tk + j * LANES, LANES)
                    hit = pred(keys_scr[r0:r0 + rb, pl.ds(off, LANES)], off + lane[:, 0:LANES], *ops_r)
                    acc = acc + hit.astype(jnp.int32)
                return acc

            parts.append(lax.fori_loop(0, nch, body, jnp.zeros((rb, LANES), jnp.int32)))
        return jnp.sum(jnp.concatenate(parts, axis=0), axis=1, keepdims=True)

    thr, jstar = _topk_cut(count, topk, lo, hi, seq, jstar_scr)

    def emit(c, carry):
        off = pl.multiple_of(c * tk, tk)
        sel = _selected(keys_scr[:, pl.ds(off, tk)], off + lane, thr, jstar)
        mask_ref[0, :, pl.ds(off, tk)] = sel.astype(jnp.int8)
        return carry

    lax.fori_loop(0, nch, emit, 0)

    def clear(c, carry):
        mask_ref[0, :, pl.ds(pl.multiple_of(c * tk, tk), tk)] = jnp.zeros((tq, tk), jnp.int8)
        return carry

    lax.fori_loop(nch, seq // tk, clear, 0)


def _block_diag_keys(ik, tk):
    b, s, d = ik.shape
    kt = jnp.swapaxes(ik.astype(BF16), 1, 2).reshape(b, d, s // tk, tk)
    z = jnp.zeros_like(kt)
    top = jnp.stack([kt, z], axis=3)
    bot = jnp.stack([z, kt], axis=3)
    return jnp.concatenate([top, bot], axis=1).reshape(b, 2 * d, 2 * s)


def _select_prompt(iq, iw, ik, topk, tq, tk):
    b, s, _ = iq.shape
    ik2 = _block_diag_keys(ik, tk)
    return pl.pallas_call(
        functools.partial(_select_kernel, tq=tq, tk=tk, topk=topk, seq=s),
        grid=(b, s // tq),
        in_specs=[pl.BlockSpec((1, tq, IDX_HEADS * IDX_DIM), lambda bb, i: (bb, i, 0)),
                  pl.BlockSpec((1, tq, IDX_HEADS), lambda bb, i: (bb, i, 0)),
                  pl.BlockSpec((1, 2 * IDX_DIM, 2 * s), lambda bb, i: (bb, 0, 0))],
        out_specs=pl.BlockSpec((1, tq, s), lambda bb, i: (bb, i, 0)),
        out_shape=jax.ShapeDtypeStruct((b, s, s), jnp.int8),
        scratch_shapes=[pltpu.VMEM((tq, s), jnp.int32), pltpu.VMEM((tq, LANES), jnp.int32),
                        pltpu.VMEM((tq, 2 * LANES), jnp.int32)],
        compiler_params=_cparams(("arbitrary", "arbitrary")), name="select_prompt",
    )(iq, iw, ik2)


def _outnorm_kernel(x_ref, mf_ref, md_ref, w1_ref, w2_ref, g_ref, b_ref, o_ref):
    y = jnp.dot(mf_ref[...].astype(BF16), w1_ref[...], preferred_element_type=F32)
    y = y + jnp.dot(md_ref[...].astype(BF16), w2_ref[...], preferred_element_type=F32)
    z = DEEPNORM_ALPHA * x_ref[...] + y
    mu = jnp.mean(z, axis=1, keepdims=True)
    zc = z - mu
    var = jnp.mean(zc * zc, axis=1, keepdims=True)
    o_ref[...] = zc * lax.rsqrt(var + LN_EPS) * g_ref[...] + b_ref[...]


def _output_and_norm(x, mix_fox, mix_dsa, w_out, ln_g, ln_b, tm):
    n = x.shape[0]
    w1 = w_out[:WIDTH].astype(BF16)
    w2 = w_out[WIDTH:].astype(BF16)
    row = lambda i: (i, 0)
    const = lambda i: (0, 0)
    return pl.pallas_call(
        _outnorm_kernel, grid=(n // tm,),
        in_specs=[pl.BlockSpec((tm, D_MODEL), row), pl.BlockSpec((tm, WIDTH), row), pl.BlockSpec((tm, WIDTH), row),
                  pl.BlockSpec((WIDTH, D_MODEL), const), pl.BlockSpec((WIDTH, D_MODEL), const),
                  pl.BlockSpec((1, D_MODEL), const), pl.BlockSpec((1, D_MODEL), const)],
        out_specs=pl.BlockSpec((tm, D_MODEL), row),
        out_shape=jax.ShapeDtypeStruct((n, D_MODEL), F32),
        compiler_params=_cparams(("arbitrary",)), name="outnorm",
    )(x, mix_fox, mix_dsa, w1, w2, ln_g[None, :].astype(F32), ln_b[None, :].astype(F32))


def _prompt_tiles(s):
    tm = min(512, s)
    t_attn = min(512, s)
    tq_sel = min(256, s)
    tk_sel = min(512, s)
    return tm, t_attn, tq_sel, tk_sel


def _prompt_group(x, w_in, b_f, w_out, ln_g, ln_b):
    b, s, d = x.shape
    tm, t_attn, tq_sel, tk_sel = _prompt_tiles(s)
    x2 = x.reshape(b * s, d)
    fq, fk, fv, logf, fg, dq, dk, dv, dg, iq, ik, iw = _project(x2, jnp.arange(s), w_in, b_f, tm)
    r3 = lambda a: a.reshape(b, s, a.shape[-1])
    c_row = _cumsum_lanes(jnp.swapaxes(r3(logf), 1, 2).reshape(b * FOX_HEADS, s)).reshape(b, FOX_HEADS, s)
    c_tok = jnp.swapaxes(c_row, 1, 2)
    mix_fox = _flash("fox", r3(fq), r3(fk), r3(fv), r3(fg), (c_tok, c_row), t_attn)
    topk = min(TOPK_MAX, s // 4)
    mask = _select_prompt(r3(iq), r3(iw), r3(ik), topk, tq_sel, tk_sel)
    mix_dsa = _flash("dsa", r3(dq), r3(dk), r3(dv), r3(dg), mask, t_attn)
    y = _output_and_norm(x2, mix_fox.reshape(b * s, WIDTH), mix_dsa.reshape(b * s, WIDTH), w_out, ln_g, ln_b, tm)
    heads = lambda a, h: a.reshape(b, 1, s, h, HEAD_DIM)
    return (y.reshape(b, s, d), heads(fk, FOX_HEADS), heads(fv, FOX_HEADS), logf.reshape(b, 1, s, FOX_HEADS),
            heads(dk, DSA_HEADS), heads(dv, DSA_HEADS), ik.reshape(b, 1, s, IDX_DIM))


def _page_specs(block, npages_per_step, n_pages):
    zeros = (0,) * (len(block) - 1)
    return [pl.BlockSpec(block, lambda b, s, pt, g=g: (pt[b * n_pages + s * npages_per_step + g],) + zeros)
            for g in range(npages_per_step)]


def _gather_rows_kernel(pt_ref, *refs):
    pages, o_ref = refs[:-1], refs[-1]
    for g, page in enumerate(pages):
        o_ref[0, g:g + 1, :] = page[0]


def _gather_rows(cache_rows, pt_flat, db, n_pages, gp):
    w = cache_rows.shape[-1]
    grid_spec = pltpu.PrefetchScalarGridSpec(
        num_scalar_prefetch=1, grid=(db, n_pages // gp), in_specs=_page_specs((1, 1, w), gp, n_pages),
        out_specs=pl.BlockSpec((1, gp, w), lambda b, s, pt: (b, s, 0)))
    return pl.pallas_call(
        _gather_rows_kernel, grid_spec=grid_spec, out_shape=jax.ShapeDtypeStruct((db, n_pages, w), F32),
        compiler_params=_cparams(("arbitrary", "arbitrary")), name="gather_rows",
    )(pt_flat, *([cache_rows] * gp))


def _dot_exact(x, binary):
    hi = x.astype(BF16)
    r1 = x - hi.astype(F32)
    mid = r1.astype(BF16)
    lo = (r1 - mid.astype(F32)).astype(BF16)
    return (jnp.dot(hi, binary, preferred_element_type=F32) + jnp.dot(mid, binary, preferred_element_type=F32)
            + jnp.dot(lo, binary, preferred_element_type=F32))


def _past_decay_kernel(l_ref, within_ref, total_ref, o_ref):
    x = l_ref[0]
    n_pages = x.shape[0]
    within = _dot_exact(x, within_ref[...])
    totals = _dot_exact(x, total_ref[...])
    later = (lax.broadcasted_iota(jnp.int32, (n_pages, n_pages), 1)
             > lax.broadcasted_iota(jnp.int32, (n_pages, n_pages), 0)).astype(BF16)
    hi = totals.astype(BF16)
    r1 = totals - hi.astype(F32)
    mid = r1.astype(BF16)
    lo = (r1 - mid.astype(F32)).astype(BF16)
    across = (jnp.dot(later, hi, preferred_element_type=F32) + jnp.dot(later, mid, preferred_element_type=F32)
              + jnp.dot(later, lo, preferred_element_type=F32))
    o_ref[0] = within + across


def _past_decay(past_logf):
    db, n_pages, w = past_logf.shape
    i = jnp.arange(w)
    same_head = (i[:, None] % FOX_HEADS) == (i[None, :] % FOX_HEADS)
    within = jnp.logical_and(same_head, i[:, None] > i[None, :]).astype(BF16)
    total = same_head.astype(BF16)
    const = lambda b: (0, 0)
    per_b = pl.BlockSpec((1, n_pages, w), lambda b: (b, 0, 0))
    return pl.pallas_call(
        _past_decay_kernel, grid=(db,),
        in_specs=[per_b, pl.BlockSpec((w, w), const), pl.BlockSpec((w, w), const)], out_specs=per_b,
        out_shape=jax.ShapeDtypeStruct((db, n_pages, w), F32),
        compiler_params=_cparams(("arbitrary",)), name="past_decay",
    )(past_logf, within, total)


def _select_sample_kernel(pt_ref, iqm_ref, wcol_ref, iknew_ref, *refs, gs, topk, past, T):
    pages = refs[:gs]
    mp_ref, mn_ref, keys_scr, jstar_scr = refs[gs:]
    s_idx = pl.program_id(1)
    iqm = iqm_ref[0]
    wcol = wcol_ref[0]
    ncols = past + PAGE_SIZE

    def scores(kblk):
        r = jnp.dot(iqm, kblk.astype(BF16), preferred_element_type=F32)
        r = wcol * jnp.maximum(r, 0.0)
        acc = r[0:T]
        for h in range(1, IDX_HEADS):
            acc = acc + r[h * T:(h + 1) * T]
        return acc

    for g in range(gs):
        off = pl.multiple_of((s_idx * gs + g) * PAGE_SIZE, PAGE_SIZE)
        keys_scr[:, pl.ds(off, PAGE_SIZE)] = _sortable_key(scores(pages[g][0]))

    @pl.when(s_idx == pl.num_programs(1) - 1)
    def _():
        t_col = lax.broadcasted_iota(jnp.int32, (T, 1), 0)
        j_row = lax.broadcasted_iota(jnp.int32, (1, PAGE_SIZE), 1)
        keys_scr[:, past:ncols] = jnp.where(j_row <= t_col, _sortable_key(scores(iknew_ref[0])), INT_MIN)
        col = lax.broadcasted_iota(jnp.int32, (1, ncols), 1)

        def count(pred, *ops):
            return jnp.sum(pred(keys_scr[...], col, *ops).astype(jnp.int32), axis=1, keepdims=True)

        assert topk <= 2 * LANES
        gmax = []
        for par in range(2):
            g = jnp.full((T, LANES), INT_MIN, jnp.int32)
            for j in range(par, ncols // LANES, 2):
                g = jnp.maximum(g, keys_scr[:, j * LANES:(j + 1) * LANES])
            gmax.append(g)
        lo, hi = _group_bounds(jnp.concatenate(gmax, axis=1))
        thr, jstar = _topk_cut(count, topk, lo, hi, ncols, jstar_scr)
        sel = _selected(keys_scr[...], col, thr, jstar).astype(F32)
        for pg in range(past // PAGE_SIZE):
            mp_ref[0, pg] = sel[:, pg * PAGE_SIZE:(pg + 1) * PAGE_SIZE]
        mn_ref[0] = sel[:, past:]


def _select_sample(iq, iw, ik_new, cache_ik, pt_flat, n_pages, topk, gs):
    db, T, _ = iq.shape
    past = n_pages * PAGE_SIZE
    rows = IDX_HEADS * T
    iqm = jnp.swapaxes(iq.reshape(db, T, IDX_HEADS, IDX_DIM), 1, 2).reshape(db, rows, IDX_DIM).astype(BF16)
    wcol = jnp.swapaxes(iw, 1, 2).reshape(db, rows, 1)
    ik_pad = jnp.swapaxes(jnp.pad(ik_new, ((0, 0), (0, PAGE_SIZE - T), (0, 0))), 1, 2)
    per_b = lambda blk: pl.BlockSpec(blk, lambda b, s, pt: (b, 0, 0))
    grid_spec = pltpu.PrefetchScalarGridSpec(
        num_scalar_prefetch=1, grid=(db, n_pages // gs),
        in_specs=[per_b((1, rows, IDX_DIM)), per_b((1, rows, 1)), per_b((1, IDX_DIM, PAGE_SIZE))]
        + _page_specs((1, IDX_DIM, PAGE_SIZE), gs, n_pages),
        out_specs=(pl.BlockSpec((1, n_pages, T, PAGE_SIZE), lambda b, s, pt: (b, 0, 0, 0)), per_b((1, T, PAGE_SIZE))),
        scratch_shapes=[pltpu.VMEM((T, past + PAGE_SIZE), jnp.int32), pltpu.VMEM((T, LANES), jnp.int32)])
    return pl.pallas_call(
        functools.partial(_select_sample_kernel, gs=gs, topk=topk, past=past, T=T), grid_spec=grid_spec,
        out_shape=(jax.ShapeDtypeStruct((db, n_pages, T, PAGE_SIZE), F32),
                   jax.ShapeDtypeStruct((db, T, PAGE_SIZE), F32)),
        compiler_params=_cparams(("arbitrary", "arbitrary")), name="select_sample",
    )(pt_flat, iqm, wcol, ik_pad, *([cache_ik] * gs))


def _decode_kernel(pt_ref, q_ref, knew_ref, vnew_ref, gate_ref, *refs, mode, G, nheads, T):
    if mode == "fox":
        pd_ref, lfcol_ref, lfrow_ref = refs[:3]
        refs = refs[3:]
    else:
        mp_ref, mn_ref = refs[:2]
        refs = refs[2:]
    k_refs, v_refs = refs[:G], refs[G:2 * G]
    o_ref, m_scr, l_scr, acc_scr = refs[2 * G:2 * G + 4]
    s_idx = pl.program_id(1)
    rows = nheads * T
    vkeys = PAGE_SIZE * nheads
    nt = (((1,), (1,)), ((), ()))

    @pl.when(s_idx == 0)
    def _():
        m_scr[...] = jnp.full(m_scr.shape, MASK_NEG, F32)
        l_scr[...] = jnp.zeros(l_scr.shape, F32)
        acc_scr[...] = jnp.zeros(acc_scr.shape, F32)

    q = q_ref[0]
    row_head = lax.broadcasted_iota(jnp.int32, (rows, 1), 0) // T
    row_tok = lax.broadcasted_iota(jnp.int32, (rows, 1), 0) % T
    lane = lax.broadcasted_iota(jnp.int32, (1, vkeys), 1)
    own = jnp.where((lane % nheads) == row_head, 0.0, MASK_NEG)
    if mode == "fox":
        ri = lax.broadcasted_iota(jnp.int32, (rows, rows), 0)
        ci = lax.broadcasted_iota(jnp.int32, (rows, rows), 1)
        tri_col = jnp.logical_and(ri // T == ci // T, ci % T <= ri % T).astype(F32)
        nc_col = jnp.dot(tri_col, lfcol_ref[0], precision=lax.Precision.HIGHEST, preferred_element_type=F32)
        ui = lax.broadcasted_iota(jnp.int32, (LANES, LANES), 0)
        uj = lax.broadcasted_iota(jnp.int32, (LANES, LANES), 1)
        tri_row = jnp.logical_and(jnp.logical_and(ui % nheads == uj % nheads, ui <= uj), uj < rows).astype(F32)
        nc_row = jnp.dot(lfrow_ref[0], tri_row, precision=lax.Precision.HIGHEST, preferred_element_type=F32)
        own_nc = own + nc_col[:, 0:1]
    else:
        expand = (lax.broadcasted_iota(jnp.int32, (PAGE_SIZE, vkeys), 1) // nheads
                  == lax.broadcasted_iota(jnp.int32, (PAGE_SIZE, vkeys), 0)).astype(BF16)
        picked = jnp.dot(mp_ref[0].reshape(G * T, PAGE_SIZE).astype(BF16), expand, preferred_element_type=F32)

    def update(s, vals):
        width = vals[0].shape[0]
        m_prev = m_scr[:, 0:1]
        m_new = jnp.maximum(m_prev, jnp.max(s, axis=1, keepdims=True))
        p = jnp.exp(s - m_new)
        alpha = jnp.exp(m_prev - m_new)
        l_scr[...] = jnp.broadcast_to(alpha * l_scr[:, 0:1] + jnp.sum(p, axis=1, keepdims=True), l_scr.shape)
        p = p.astype(BF16)
        pv = jnp.dot(p[:, 0:width], vals[0], preferred_element_type=F32)
        for g in range(1, len(vals)):
            pv = pv + jnp.dot(p[:, g * width:(g + 1) * width], vals[g], preferred_element_type=F32)
        acc_scr[...] = alpha * acc_scr[...] + pv
        m_scr[...] = jnp.broadcast_to(m_new, m_scr.shape)

    parts = []
    for g in range(G):
        s = lax.dot_general(q, k_refs[g][0].astype(BF16), nt, preferred_element_type=F32) * ATTN_SCALE
        if mode == "fox":
            s = s + (own_nc + pd_ref[0, g:g + 1, :])
        else:
            keep = jnp.concatenate([picked[g * T:(g + 1) * T]] * nheads, axis=0) > 0.5
            s = jnp.where(keep, s + own, MASK_NEG)
        parts.append(s)
    update(jnp.concatenate(parts, axis=1), [v[0].astype(BF16) for v in v_refs])

    @pl.when(s_idx == pl.num_programs(1) - 1)
    def _():
        sn = lax.dot_general(q, knew_ref[0].astype(BF16), nt, preferred_element_type=F32) * ATTN_SCALE
        lane_n = lax.broadcasted_iota(jnp.int32, (1, LANES), 1)
        if mode == "fox":
            keep = (lane_n // nheads) <= row_tok
            sn = sn + (nc_col[:, 0:1] - nc_row[0:1, :])
        else:
            keep = jnp.concatenate([jnp.dot(mn_ref[0].astype(BF16), expand[:, 0:LANES],
                                            preferred_element_type=F32)] * nheads, axis=0) > 0.5
        sn = jnp.where(keep, sn + own[:, 0:LANES], MASK_NEG)
        update(sn, [vnew_ref[0].astype(BF16)])
        o = acc_scr[...] / l_scr[:, 0:1]
        for h in range(nheads):
            sl = slice(h * HEAD_DIM, (h + 1) * HEAD_DIM)
            g = gate_ref[0, :, sl]
            o_ref[0, :, sl] = o[h * T:(h + 1) * T] * (g / (1.0 + jnp.exp(-g)))


def _decode(mode, q, k_new, v_new, gate, extra, cache_k, cache_v, pt_flat, n_pages, G):
    db, T, w = q.shape
    nheads = w // HEAD_DIM
    rows = nheads * T
    assert rows <= LANES and nheads * HEAD_DIM == w
    vkeys = PAGE_SIZE * nheads
    q_rows = jnp.swapaxes(q.reshape(db, T, nheads, HEAD_DIM), 1, 2).reshape(db, rows, HEAD_DIM).astype(BF16)
    new_rows = lambda a: jnp.pad(a.reshape(db, rows, HEAD_DIM), ((0, 0), (0, LANES - rows), (0, 0)))
    per_b = lambda blk: pl.BlockSpec(blk, lambda b, s, pt: (b,) + (0,) * (len(blk) - 1))
    step = lambda blk: pl.BlockSpec(blk, lambda b, s, pt: (b, s) + (0,) * (len(blk) - 2))
    in_specs = [per_b((1, rows, HEAD_DIM)), per_b((1, LANES, HEAD_DIM)), per_b((1, LANES, HEAD_DIM)),
                per_b((1, T, w))]
    if mode == "fox":
        pd, logf_new = extra
        lf_col = jnp.broadcast_to(jnp.swapaxes(logf_new, 1, 2).reshape(db, rows, 1), (db, rows, LANES))
        lf_row = jnp.broadcast_to(jnp.pad(logf_new.reshape(db, 1, rows), ((0, 0), (0, 0), (0, LANES - rows))),
                                  (db, 8, LANES))
        in_specs += [step((1, G, vkeys)), per_b((1, rows, LANES)), per_b((1, 8, LANES))]
        args = (pd, lf_col, lf_row)
    else:
        mask_past, mask_new = extra
        in_specs += [step((1, G, T, PAGE_SIZE)), per_b((1, T, PAGE_SIZE))]
        args = (mask_past, mask_new)
    in_specs += _page_specs((1, vkeys, HEAD_DIM), G, n_pages) + _page_specs((1, vkeys, HEAD_DIM), G, n_pages)
    grid_spec = pltpu.PrefetchScalarGridSpec(
        num_scalar_prefetch=1, grid=(db, n_pages // G), in_specs=in_specs, out_specs=per_b((1, T, w)),
        scratch_shapes=[pltpu.VMEM((rows, LANES), F32), pltpu.VMEM((rows, LANES), F32),
                        pltpu.VMEM((rows, HEAD_DIM), F32)])
    return pl.pallas_call(
        functools.partial(_decode_kernel, mode=mode, G=G, nheads=nheads, T=T), grid_spec=grid_spec,
        out_shape=jax.ShapeDtypeStruct((db, T, w), F32),
        compiler_params=_cparams(("arbitrary", "arbitrary")), name=f"decode_{mode}",
    )(pt_flat, q_rows, new_rows(k_new), new_rows(v_new), gate, *args, *([cache_k] * G), *([cache_v] * G))


def _sample_group(x, caches, page_table, w_in, b_f, w_out, ln_g, ln_b):
    cache_fox_k, cache_fox_v, cache_fox_logf, cache_dsa_k, cache_dsa_v, cache_idx_k = caches
    db, T, d = x.shape
    n_pages = page_table.shape[1]
    past = n_pages * PAGE_SIZE
    nphys = cache_fox_k.shape[0]
    G = min(16, n_pages)
    pt_flat = page_table.reshape(-1).astype(jnp.int32)
    x2 = x.reshape(db * T, d)
    pos = jnp.tile(past + jnp.arange(T), db)
    fq, fk, fv, logf, fg, dq, dk, dv, dg, iq, ik, iw = _project(x2, pos, w_in, b_f, db * T)
    r3 = lambda a: a.reshape(db, T, a.shape[-1])
    kv_rows = lambda c: c.reshape(nphys, PAGE_SIZE * c.shape[3], HEAD_DIM)
    past_logf = _gather_rows(cache_fox_logf.reshape(nphys, 1, PAGE_SIZE * FOX_HEADS), pt_flat, db, n_pages,
                             min(32, n_pages))
    pd = _past_decay(past_logf)
    mix_fox = _decode("fox", r3(fq), r3(fk), r3(fv), r3(fg), (pd, r3(logf)),
                      kv_rows(cache_fox_k), kv_rows(cache_fox_v), pt_flat, n_pages, G)
    topk = min(TOPK_MAX, (past + T) // 4)
    ik_pages = jnp.swapaxes(cache_idx_k.reshape(nphys, PAGE_SIZE, IDX_DIM), 1, 2)
    masks = _select_sample(r3(iq), r3(iw), r3(ik), ik_pages, pt_flat, n_pages, topk, min(32, n_pages))
    mix_dsa = _decode("dsa", r3(dq), r3(dk), r3(dv), r3(dg), masks,
                      kv_rows(cache_dsa_k), kv_rows(cache_dsa_v), pt_flat, n_pages, G)
    y = _output_and_norm(x2, mix_fox.reshape(db * T, WIDTH), mix_dsa.reshape(db * T, WIDTH), w_out, ln_g, ln_b,
                         db * T)
    heads = lambda a, h: a.reshape(db, 1, T, h, HEAD_DIM)
    return (y.reshape(db, T, d), heads(fk, FOX_HEADS), heads(fv, FOX_HEADS), logf.reshape(db, 1, T, FOX_HEADS),
            heads(dk, DSA_HEADS), heads(dv, DSA_HEADS), ik.reshape(db, 1, T, IDX_DIM))


def kernel(x_prompt, x_sample, cache_fox_k, cache_fox_v, cache_fox_logf, cache_dsa_k, cache_dsa_v, cache_idx_k,
           page_table, w_in, b_f, w_out, ln_g, ln_b):
    assert w_in.shape[0] == DEPTH and cache_fox_k.shape[1] == DEPTH
    params = (w_in[0], b_f[0], w_out[0], ln_g[0], ln_b[0])
    caches = (cache_fox_k, cache_fox_v, cache_fox_logf, cache_dsa_k, cache_dsa_v, cache_idx_k)
    p = _prompt_group(x_prompt, *params)
    s = _sample_group(x_sample, caches, page_table, *params)
    return (p[0], s[0], *p[1:], *s[1:])
```

```python
import functools
import math

import jax
import jax.numpy as jnp
import numpy as np
from jax import lax
from jax.experimental import pallas as pl
from jax.experimental.pallas import tpu as pltpu

F32 = jnp.float32
BF16 = jnp.bfloat16

D_MODEL = 2048
HEAD_DIM = 128
FOX_HEADS = 8
DSA_HEADS = 8
WIDTH = FOX_HEADS * HEAD_DIM
IDX_HEADS = 16
IDX_DIM = 64
TOPK_MAX = 256
PAGE_SIZE = 128
ROPE_THETA = 10000.0
LN_EPS = 1e-5
ATTN_SCALE = HEAD_DIM ** -0.5
LOG2E = math.log2(math.e)
IDX_W_SCALE = IDX_HEADS ** -0.5 * IDX_DIM ** -0.5
DEPTH = 1
DEEPNORM_ALPHA = (2.0 * DEPTH) ** 0.25
WIDTHS = (WIDTH, WIDTH, WIDTH, FOX_HEADS, WIDTH, WIDTH, WIDTH, WIDTH, WIDTH,
          IDX_HEADS * IDX_DIM, IDX_DIM, IDX_HEADS)
OFFS = tuple(int(v) for v in np.cumsum((0,) + WIDTHS))

LANES = 128
VMEM_LIMIT_BYTES = 56 * 1024 * 1024
MASK_NEG = -1e30
SOFTMAX_ROWS = 64
HEAD_GROUP = 2
INT_MIN = -(2 ** 31)


def _cparams(sem):
    return pltpu.CompilerParams(dimension_semantics=sem, vmem_limit_bytes=VMEM_LIMIT_BYTES)


def _rope_tables(pos, dim):
    half = dim // 2
    inv = ROPE_THETA ** (-jnp.arange(half, dtype=F32) * 2.0 / dim)
    ang = pos.astype(F32)[:, None] * inv[None, :]
    cos, sin = jnp.cos(ang), jnp.sin(ang)
    zero = jnp.zeros_like(sin)
    if dim == LANES:
        return (jnp.concatenate([cos, cos], 1), jnp.concatenate([-sin, sin], 1))
    return (jnp.concatenate([cos, cos, cos, cos], 1),
            jnp.concatenate([-sin, zero, -sin, zero], 1),
            jnp.concatenate([zero, sin, zero, sin], 1))


def _rope128(blk, cos, sin):
    return blk * cos + pltpu.roll(blk, 64, axis=1) * sin


def _rope64(blk, cos, sin_lo, sin_hi):
    return blk * cos + pltpu.roll(blk, 96, axis=1) * sin_lo + pltpu.roll(blk, 32, axis=1) * sin_hi


def _proj_kernel(x_ref, *refs, kind, nseg):
    xb = x_ref[...]
    w_refs, refs = refs[:nseg], refs[nseg:]
    if kind == "plain":
        for w_ref, o_ref in zip(w_refs, refs):
            for c in range(0, WIDTH, 256):
                o_ref[:, c:c + 256] = jnp.dot(xb, w_ref[:, c:c + 256], preferred_element_type=F32)
    elif kind == "rope128":
        cos, sin = refs[0][...], refs[1][...]
        for w_ref, o_ref in zip(w_refs, refs[2:]):
            for c in range(0, WIDTH, 256):
                acc = jnp.dot(xb, w_ref[:, c:c + 256], preferred_element_type=F32)
                for h in range(2):
                    o_ref[:, c + h * 128:c + (h + 1) * 128] = _rope128(acc[:, h * 128:(h + 1) * 128], cos, sin)
    elif kind == "rope64":
        cos, slo, shi = refs[0][...], refs[1][...], refs[2][...]
        for w_ref, o_ref in zip(w_refs, refs[3:]):
            for c in range(0, WIDTH, 256):
                acc = jnp.dot(xb, w_ref[:, c:c + 256], preferred_element_type=F32)
                for h in range(2):
                    o_ref[:, c + h * 128:c + (h + 1) * 128] = _rope64(acc[:, h * 128:(h + 1) * 128], cos, slo, shi)
    else:
        (w_ref,) = w_refs
        cos_ref, slo_ref, shi_ref, bf_ref, ik_ref, logf_ref, iw_ref = refs
        acc = jnp.dot(xb, w_ref[...], preferred_element_type=F32)
        ik = _rope64(acc[:, 0:128], cos_ref[...], slo_ref[...], shi_ref[...])
        ik_ref[...] = ik[:, :IDX_DIM]
        z = acc[:, 128:256] + bf_ref[...]
        logf = jnp.minimum(z, 0.0) - jnp.log(1.0 + jnp.exp(-jnp.abs(z)))
        logf_ref[...] = logf[:, :FOX_HEADS]
        iw_ref[...] = (acc[:, 256:384] * IDX_W_SCALE)[:, :IDX_HEADS]


def _project_segments(x, ws, kind, tables, tm, extra=None):
    n = x.shape[0]
    grid = (n // tm,)
    tab_blocks = tables[0].shape[0] // tm if tables else 1
    row = lambda i: (i, 0)
    tab = lambda i: (i % tab_blocks, 0)
    in_specs = [pl.BlockSpec((tm, D_MODEL), row)]
    in_specs += [pl.BlockSpec((D_MODEL, w.shape[1]), lambda i: (0, 0)) for w in ws]
    in_specs += [pl.BlockSpec((tm, LANES), tab) for _ in tables]
    args = [x, *ws, *tables]
    if kind == "small":
        in_specs.append(pl.BlockSpec((1, LANES), lambda i: (0, 0)))
        args.append(extra)
        out_shape = (jax.ShapeDtypeStruct((n, IDX_DIM), F32), jax.ShapeDtypeStruct((n, FOX_HEADS), F32),
                     jax.ShapeDtypeStruct((n, IDX_HEADS), F32))
        out_specs = (pl.BlockSpec((tm, IDX_DIM), row), pl.BlockSpec((tm, FOX_HEADS), row),
                     pl.BlockSpec((tm, IDX_HEADS), row))
    else:
        out_shape = tuple(jax.ShapeDtypeStruct((n, w.shape[1]), F32) for w in ws)
        out_specs = tuple(pl.BlockSpec((tm, w.shape[1]), row) for w in ws)
    return pl.pallas_call(
        functools.partial(_proj_kernel, kind=kind, nseg=len(ws)),
        grid=grid, in_specs=in_specs, out_specs=out_specs, out_shape=out_shape,
        compiler_params=_cparams(("arbitrary",)), name=f"proj_{kind}",
    )(*args)


def _project(x, pos_rows, w_in, b_f, tm):
    seg = lambda i: w_in[:, OFFS[i]:OFFS[i + 1]].astype(BF16)
    xb = x.astype(BF16)
    t128 = _rope_tables(pos_rows, HEAD_DIM)
    t64 = _rope_tables(pos_rows, IDX_DIM)
    fq, fk = _project_segments(xb, [seg(0), seg(1)], "plain", (), tm)
    fv, fg = _project_segments(xb, [seg(2), seg(4)], "plain", (), tm)
    dq, dk = _project_segments(xb, [seg(5), seg(6)], "rope128", t128, tm)
    dv, dg = _project_segments(xb, [seg(7), seg(8)], "plain", (), tm)
    (iq,) = _project_segments(xb, [seg(9)], "rope64", t64, tm)
    pad = lambda a: jnp.pad(a, ((0, 0), (0, LANES - a.shape[1])))
    w_small = jnp.concatenate([pad(w_in[:, OFFS[10]:OFFS[11]]), pad(w_in[:, OFFS[3]:OFFS[4]]),
                               pad(w_in[:, OFFS[11]:OFFS[12]])], axis=1).astype(BF16)
    ik, logf, iw = _project_segments(xb, [w_small], "small", t64, tm, extra=pad(b_f[None, :].astype(F32)))
    return fq, fk, fv, logf, fg, dq, dk, dv, dg, iq, ik, iw


def _cumsum_kernel(x_ref, o_ref, *, suffix):
    rows, n = x_ref.shape
    r = lax.broadcasted_iota(jnp.int32, (LANES, LANES), 0)
    c = lax.broadcasted_iota(jnp.int32, (LANES, LANES), 1)
    tri = ((r > c) if suffix else (r <= c)).astype(F32)
    carry = jnp.zeros((rows, 1), F32)
    nchunks = n // LANES
    for step in range(nchunks):
        j = nchunks - 1 - step if suffix else step
        x = x_ref[:, j * LANES:(j + 1) * LANES]
        blk = jnp.dot(x, tri, precision=lax.Precision.HIGHEST, preferred_element_type=F32) + carry
        o_ref[:, j * LANES:(j + 1) * LANES] = blk
        carry = (blk[:, 0:1] + x[:, 0:1]) if suffix else blk[:, LANES - 1:LANES]


def _cumsum_lanes(x, suffix=False):
    rows, n = x.shape
    tr = min(rows, 64)
    return pl.pallas_call(
        functools.partial(_cumsum_kernel, suffix=suffix), grid=(rows // tr,),
        in_specs=[pl.BlockSpec((tr, n), lambda i: (i, 0))], out_specs=pl.BlockSpec((tr, n), lambda i: (i, 0)),
        out_shape=jax.ShapeDtypeStruct(x.shape, F32), compiler_params=_cparams(("arbitrary",)),
        name="cumsum_suffix" if suffix else "cumsum")(x)


def _flash_kernel(qi_tab, ki_tab, q_ref, k_ref, v_ref, g_ref, *refs, mode, T, nheads):
    if mode == "fox":
        ctok_ref, crow_ref, o_ref, m_scr, acc_scr, bias_scr = refs
    else:
        mask_ref, o_ref, m_scr, acc_scr, bias_scr = refs
    t = pl.program_id(1)
    qi, ki = qi_tab[t], ki_tab[t]

    @pl.when(ki == 0)
    def _():
        m_scr[...] = jnp.full(m_scr.shape, MASK_NEG, F32)
        acc_scr[...] = jnp.zeros(acc_scr.shape, F32)

    nt = (((1,), (1,)), ((), ()))
    subs = [slice(r, r + SOFTMAX_ROWS) for r in range(0, T, SOFTMAX_ROWS)]

    def step(diag):
        if mode == "dsa":
            for rs in subs:
                bias_scr[rs] = jnp.where(mask_ref[0, rs].astype(jnp.int32) != 0, 0.0, MASK_NEG)
        elif diag:
            for rs in subs:
                col = lax.broadcasted_iota(jnp.int32, (SOFTMAX_ROWS, T), 1)
                row = rs.start + lax.broadcasted_iota(jnp.int32, (SOFTMAX_ROWS, T), 0)
                bias_scr[rs] = jnp.where(col <= row, 0.0, MASK_NEG)
        biased = mode == "dsa" or diag

        for h0 in range(0, nheads, HEAD_GROUP):
            heads = range(h0, min(h0 + HEAD_GROUP, nheads))
            cols = {h: slice(h * HEAD_DIM, (h + 1) * HEAD_DIM) for h in heads}
            s = {h: lax.dot_general(q_ref[0, :, cols[h]].astype(BF16), k_ref[0, :, cols[h]].astype(BF16), nt,
                                    preferred_element_type=F32) * (ATTN_SCALE * LOG2E) for h in heads}
            if mode == "fox":
                s = {h: s[h] + (ctok_ref[0, :, h:h + 1] * LOG2E - crow_ref[0, h:h + 1, :] * LOG2E) for h in heads}
            if biased:
                s = {h: s[h] + bias_scr[...] for h in heads}
            m_prev = {h: m_scr[h] for h in heads}
            m_new = {h: jnp.maximum(m_prev[h], jnp.broadcast_to(jnp.max(s[h], axis=1, keepdims=True), (T, LANES)))
                     for h in heads}
            p = {h: jnp.exp2(s[h] - jnp.concatenate([m_new[h]] * (T // LANES), axis=1)).astype(BF16) for h in heads}
            alpha = {h: jnp.exp2(m_prev[h] - m_new[h]) for h in heads}
            ones = jnp.ones((T, HEAD_DIM), BF16)
            pv = {h: jnp.dot(p[h], jnp.concatenate([v_ref[0, :, cols[h]].astype(BF16), ones], axis=1),
                             preferred_element_type=F32) for h in heads}
            for h in heads:
                acc_scr[h] = jnp.concatenate([alpha[h]] * 2, axis=1) * acc_scr[h] + pv[h]
                m_scr[h] = m_new[h]

    if mode == "dsa":
        step(False)
    else:
        pl.when(ki == qi)(functools.partial(step, True))
        pl.when(ki != qi)(functools.partial(step, False))

    @pl.when(ki == qi)
    def _():
        for h in range(nheads):
            sl = slice(h * HEAD_DIM, (h + 1) * HEAD_DIM)
            g = g_ref[0, :, sl]
            o = acc_scr[h, :, 0:HEAD_DIM] / acc_scr[h, :, HEAD_DIM:2 * HEAD_DIM]
            o_ref[0, :, sl] = (o * (g / (1.0 + jnp.exp(-g)))).astype(BF16)


def _flash(mode, q, k, v, gate, extra, T):
    b, s, w = q.shape
    nheads = w // HEAD_DIM
    nb = s // T
    pairs = [(i, j) for i in range(nb) for j in range(i + 1)]
    qi_tab = jnp.asarray([p[0] for p in pairs], jnp.int32)
    ki_tab = jnp.asarray([p[1] for p in pairs], jnp.int32)
    qmap = lambda bb, t, qt, kt: (bb, qt[t], 0)
    kmap = lambda bb, t, qt, kt: (bb, kt[t], 0)
    in_specs = [pl.BlockSpec((1, T, w), qmap), pl.BlockSpec((1, T, w), kmap), pl.BlockSpec((1, T, w), kmap),
                pl.BlockSpec((1, T, w), qmap)]
    if mode == "fox":
        c_tok, c_row = extra
        in_specs += [pl.BlockSpec((1, T, nheads), qmap),
                     pl.BlockSpec((1, nheads, T), lambda bb, t, qt, kt: (bb, 0, kt[t]))]
        args = (c_tok, c_row)
    else:
        in_specs += [pl.BlockSpec((1, T, T), lambda bb, t, qt, kt: (bb, qt[t], kt[t]))]
        args = (extra,)
    grid_spec = pltpu.PrefetchScalarGridSpec(
        num_scalar_prefetch=2, grid=(b, len(pairs)), in_specs=in_specs,
        out_specs=pl.BlockSpec((1, T, w), qmap),
        scratch_shapes=[pltpu.VMEM((nheads, T, LANES), F32), pltpu.VMEM((nheads, T, 2 * HEAD_DIM), F32),
                        pltpu.VMEM((T, T), F32)])
    return pl.pallas_call(
        functools.partial(_flash_kernel, mode=mode, T=T, nheads=nheads),
        grid_spec=grid_spec, out_shape=jax.ShapeDtypeStruct((b, s, w), BF16),
        compiler_params=_cparams(("arbitrary", "arbitrary")), name=f"flash_{mode}",
    )(qi_tab, ki_tab, q, k, v, gate, *args)


def _sortable_key(x):
    bits = pltpu.bitcast(x, jnp.int32)
    return bits ^ ((bits >> 31) & 0x7FFFFFFF)


def _group_bounds(group_max):
    return jnp.min(group_max, axis=1, keepdims=True), jnp.max(group_max, axis=1, keepdims=True)


def _kth_threshold(count_ge, topk, lo, hi):
    def cond(carry):
        return carry[-1] > 0

    def body(carry):
        lo, hi, exact, _ = carry
        mid = (lo | hi) - ((lo ^ hi) >> 1)
        n = count_ge(mid)
        live = jnp.logical_and(exact == 0, lo < hi)
        up = jnp.logical_and(live, n >= topk)
        lo = jnp.where(up, mid, lo)
        hi = jnp.where(jnp.logical_and(live, n < topk), mid - 1, hi)
        exact = jnp.where(jnp.logical_and(up, n == topk), 1, exact)
        return lo, hi, exact, jnp.sum(jnp.logical_and(exact == 0, lo < hi).astype(jnp.int32))

    e0 = jnp.zeros(lo.shape, jnp.int32)
    lo, _, exact, _ = lax.while_loop(cond, body, (lo, hi, e0, jnp.sum((lo < hi).astype(jnp.int32))))
    return lo, jnp.where(exact == 1, topk, -1)


def _topk_cut(count, topk, lo, hi, ncols, jstar_scr):
    rows = lo.shape[0]
    thr, n_ge_known = _kth_threshold(lambda cand: count(lambda key, col, c: key >= c, cand), topk, lo, hi)
    real = thr > INT_MIN
    jstar_scr[...] = jnp.broadcast_to(jnp.where(real, ncols - 1, -1).astype(jnp.int32), jstar_scr.shape)

    @pl.when(jnp.min(n_ge_known) < 0)
    def _():
        n_ge = count(lambda key, col, t: key >= t, thr)
        tie = jnp.logical_and(real, n_ge > topk)

        @pl.when(jnp.max(tie.astype(jnp.int32)) > 0)
        def _():
            need = topk - count(lambda key, col, t: key > t, thr)
            nbits = max(1, (ncols - 1).bit_length())

            def body(b, j):
                cand = j | (jnp.int32(1) << (nbits - 1 - b))
                below = count(lambda key, col, t, c: jnp.logical_and(key == t, col < c), thr, cand)
                return jnp.where(below < need, cand, j)

            j = lax.fori_loop(0, nbits, body, jnp.zeros((rows, 1), jnp.int32))
            jstar_scr[...] = jnp.broadcast_to(jnp.where(tie, j, jstar_scr[:, 0:1]), jstar_scr.shape)

    return thr, jstar_scr[:, 0:1]


def _selected(key, col, thr, jstar):
    return jnp.where(key == thr, (col <= jstar).astype(jnp.int32), (key > thr).astype(jnp.int32))


def _select_kernel(iq_ref, iw_ref, ik2_ref, mask_ref, keys_scr, jstar_scr, gmax_scr, *, tq, tk, topk, seq):
    i = pl.program_id(1)
    nch = ((i + 1) * tq + tk - 1) // tk
    iqb = iq_ref[0].astype(BF16)
    w = iw_ref[0]
    row = i * tq + lax.broadcasted_iota(jnp.int32, (tq, 1), 0)
    lane = lax.broadcasted_iota(jnp.int32, (1, tk), 1)

    def score_chunk(c, carry):
        off = pl.multiple_of(c * tk, tk)
        k2 = ik2_ref[0, :, pl.ds(pl.multiple_of(c * 2 * tk, 2 * tk), 2 * tk)]
        acc = jnp.zeros((tq, tk), F32)
        for pr in range(IDX_HEADS // 2):
            r = jnp.dot(iqb[:, pr * 128:(pr + 1) * 128], k2, preferred_element_type=F32)
            acc = acc + w[:, 2 * pr:2 * pr + 1] * jnp.maximum(r[:, :tk], 0.0)
            acc = acc + w[:, 2 * pr + 1:2 * pr + 2] * jnp.maximum(r[:, tk:], 0.0)
        key = jnp.where(off + lane <= row, _sortable_key(acc), INT_MIN)
        keys_scr[:, pl.ds(off, tk)] = key
        for par in range(2):
            g = gmax_scr[:, par * LANES:(par + 1) * LANES]
            for j in range(par, tk // LANES, 2):
                g = jnp.maximum(g, key[:, j * LANES:(j + 1) * LANES])
            gmax_scr[:, par * LANES:(par + 1) * LANES] = g
        return carry

    assert tk % (2 * LANES) == 0 and topk <= 2 * LANES
    gmax_scr[...] = jnp.full(gmax_scr.shape, INT_MIN, jnp.int32)
    lax.fori_loop(0, nch, score_chunk, 0)
    lo, hi = _group_bounds(gmax_scr[...])

    rb = min(tq, 128)

    def count(pred, *ops):
        parts = []
        for r0 in range(0, tq, rb):
            ops_r = [jnp.broadcast_to(o[r0:r0 + rb], (rb, LANES)) for o in ops]

            def body(c, acc, r0=r0, ops_r=ops_r):
                for j in range(tk // LANES):
                    off = pl.multiple_of(c * tk + j * LANES, LANES)
                    hit = pred(keys_scr[r0:r0 + rb, pl.ds(off, LANES)], off + lane[:, 0:LANES], *ops_r)
                    acc = acc + hit.astype(jnp.int32)
                return acc

            parts.append(lax.fori_loop(0, nch, body, jnp.zeros((rb, LANES), jnp.int32)))
        return jnp.sum(jnp.concatenate(parts, axis=0), axis=1, keepdims=True)

    thr, jstar = _topk_cut(count, topk, lo, hi, seq, jstar_scr)

    def emit(c, carry):
        off = pl.multiple_of(c * tk, tk)
        sel = _selected(keys_scr[:, pl.ds(off, tk)], off + lane, thr, jstar)
        mask_ref[0, :, pl.ds(off, tk)] = sel.astype(jnp.int8)
        return carry

    lax.fori_loop(0, nch, emit, 0)

    def clear(c, carry):
        mask_ref[0, :, pl.ds(pl.multiple_of(c * tk, tk), tk)] = jnp.zeros((tq, tk), jnp.int8)
        return carry

    lax.fori_loop(nch, seq // tk, clear, 0)


def _block_diag_keys(ik, tk):
    b, s, d = ik.shape
    kt = jnp.swapaxes(ik.astype(BF16), 1, 2).reshape(b, d, s // tk, tk)
    z = jnp.zeros_like(kt)
    top = jnp.stack([kt, z], axis=3)
    bot = jnp.stack([z, kt], axis=3)
    return jnp.concatenate([top, bot], axis=1).reshape(b, 2 * d, 2 * s)


def _select_prompt(iq, iw, ik, topk, tq, tk):
    b, s, _ = iq.shape
    ik2 = _block_diag_keys(ik, tk)
    return pl.pallas_call(
        functools.partial(_select_kernel, tq=tq, tk=tk, topk=topk, seq=s),
        grid=(b, s // tq),
        in_specs=[pl.BlockSpec((1, tq, IDX_HEADS * IDX_DIM), lambda bb, i: (bb, i, 0)),
                  pl.BlockSpec((1, tq, IDX_HEADS), lambda bb, i: (bb, i, 0)),
                  pl.BlockSpec((1, 2 * IDX_DIM, 2 * s), lambda bb, i: (bb, 0, 0))],
        out_specs=pl.BlockSpec((1, tq, s), lambda bb, i: (bb, i, 0)),
        out_shape=jax.ShapeDtypeStruct((b, s, s), jnp.int8),
        scratch_shapes=[pltpu.VMEM((tq, s), jnp.int32), pltpu.VMEM((tq, LANES), jnp.int32),
                        pltpu.VMEM((tq, 2 * LANES), jnp.int32)],
        compiler_params=_cparams(("arbitrary", "arbitrary")), name="select_prompt",
    )(iq, iw, ik2)


def _outnorm_kernel(x_ref, mf_ref, md_ref, w1_ref, w2_ref, g_ref, b_ref, o_ref):
    y = jnp.dot(mf_ref[...].astype(BF16), w1_ref[...], preferred_element_type=F32)
    y = y + jnp.dot(md_ref[...].astype(BF16), w2_ref[...], preferred_element_type=F32)
    z = DEEPNORM_ALPHA * x_ref[...] + y
    mu = jnp.mean(z, axis=1, keepdims=True)
    zc = z - mu
    var = jnp.mean(zc * zc, axis=1, keepdims=True)
    o_ref[...] = zc * lax.rsqrt(var + LN_EPS) * g_ref[...] + b_ref[...]


def _output_and_norm(x, mix_fox, mix_dsa, w_out, ln_g, ln_b, tm):
    n = x.shape[0]
    w1 = w_out[:WIDTH].astype(BF16)
    w2 = w_out[WIDTH:].astype(BF16)
    row = lambda i: (i, 0)
    const = lambda i: (0, 0)
    return pl.pallas_call(
        _outnorm_kernel, grid=(n // tm,),
        in_specs=[pl.BlockSpec((tm, D_MODEL), row), pl.BlockSpec((tm, WIDTH), row), pl.BlockSpec((tm, WIDTH), row),
                  pl.BlockSpec((WIDTH, D_MODEL), const), pl.BlockSpec((WIDTH, D_MODEL), const),
                  pl.BlockSpec((1, D_MODEL), const), pl.BlockSpec((1, D_MODEL), const)],
        out_specs=pl.BlockSpec((tm, D_MODEL), row),
        out_shape=jax.ShapeDtypeStruct((n, D_MODEL), F32),
        compiler_params=_cparams(("arbitrary",)), name="outnorm",
    )(x, mix_fox, mix_dsa, w1, w2, ln_g[None, :].astype(F32), ln_b[None, :].astype(F32))


def _prompt_tiles(s):
    tm = min(512, s)
    t_attn = min(512, s)
    tq_sel = min(512, s)
    tk_sel = min(512, s)
    return tm, t_attn, tq_sel, tk_sel


def _prompt_group(x, w_in, b_f, w_out, ln_g, ln_b):
    b, s, d = x.shape
    tm, t_attn, tq_sel, tk_sel = _prompt_tiles(s)
    x2 = x.reshape(b * s, d)
    fq, fk, fv, logf, fg, dq, dk, dv, dg, iq, ik, iw = _project(x2, jnp.arange(s), w_in, b_f, tm)
    r3 = lambda a: a.reshape(b, s, a.shape[-1])
    c_row = _cumsum_lanes(jnp.swapaxes(r3(logf), 1, 2).reshape(b * FOX_HEADS, s)).reshape(b, FOX_HEADS, s)
    c_tok = jnp.swapaxes(c_row, 1, 2)
    mix_fox = _flash("fox", r3(fq), r3(fk), r3(fv), r3(fg), (c_tok, c_row), t_attn)
    topk = min(TOPK_MAX, s // 4)
    mask = _select_prompt(r3(iq), r3(iw), r3(ik), topk, tq_sel, tk_sel)
    mix_dsa = _flash("dsa", r3(dq), r3(dk), r3(dv), r3(dg), mask, t_attn)
    y = _output_and_norm(x2, mix_fox.reshape(b * s, WIDTH), mix_dsa.reshape(b * s, WIDTH), w_out, ln_g, ln_b, tm)
    heads = lambda a, h: a.reshape(b, 1, s, h, HEAD_DIM)
    return (y.reshape(b, s, d), heads(fk, FOX_HEADS), heads(fv, FOX_HEADS), logf.reshape(b, 1, s, FOX_HEADS),
            heads(dk, DSA_HEADS), heads(dv, DSA_HEADS), ik.reshape(b, 1, s, IDX_DIM))


def _page_specs(block, npages_per_step, n_pages):
    zeros = (0,) * (len(block) - 1)
    return [pl.BlockSpec(block, lambda b, s, pt, g=g: (pt[b * n_pages + s * npages_per_step + g],) + zeros)
            for g in range(npages_per_step)]


def _gather_rows_kernel(pt_ref, *refs):
    pages, o_ref = refs[:-1], refs[-1]
    for g, page in enumerate(pages):
        o_ref[0, g:g + 1, :] = page[0]


def _gather_rows(cache_rows, pt_flat, db, n_pages, gp):
    w = cache_rows.shape[-1]
    grid_spec = pltpu.PrefetchScalarGridSpec(
        num_scalar_prefetch=1, grid=(db, n_pages // gp), in_specs=_page_specs((1, 1, w), gp, n_pages),
        out_specs=pl.BlockSpec((1, gp, w), lambda b, s, pt: (b, s, 0)))
    return pl.pallas_call(
        _gather_rows_kernel, grid_spec=grid_spec, out_shape=jax.ShapeDtypeStruct((db, n_pages, w), F32),
        compiler_params=_cparams(("arbitrary", "arbitrary")), name="gather_rows",
    )(pt_flat, *([cache_rows] * gp))


def _dot_exact(x, binary):
    hi = x.astype(BF16)
    r1 = x - hi.astype(F32)
    mid = r1.astype(BF16)
    lo = (r1 - mid.astype(F32)).astype(BF16)
    return (jnp.dot(hi, binary, preferred_element_type=F32) + jnp.dot(mid, binary, preferred_element_type=F32)
            + jnp.dot(lo, binary, preferred_element_type=F32))


def _past_decay_kernel(l_ref, within_ref, total_ref, o_ref):
    x = l_ref[0]
    n_pages = x.shape[0]
    within = _dot_exact(x, within_ref[...])
    totals = _dot_exact(x, total_ref[...])
    later = (lax.broadcasted_iota(jnp.int32, (n_pages, n_pages), 1)
             > lax.broadcasted_iota(jnp.int32, (n_pages, n_pages), 0)).astype(BF16)
    hi = totals.astype(BF16)
    r1 = totals - hi.astype(F32)
    mid = r1.astype(BF16)
    lo = (r1 - mid.astype(F32)).astype(BF16)
    across = (jnp.dot(later, hi, preferred_element_type=F32) + jnp.dot(later, mid, preferred_element_type=F32)
              + jnp.dot(later, lo, preferred_element_type=F32))
    o_ref[0] = within + across


def _past_decay(past_logf):
    db, n_pages, w = past_logf.shape
    i = jnp.arange(w)
    same_head = (i[:, None] % FOX_HEADS) == (i[None, :] % FOX_HEADS)
    within = jnp.logical_and(same_head, i[:, None] > i[None, :]).astype(BF16)
    total = same_head.astype(BF16)
    const = lambda b: (0, 0)
    per_b = pl.BlockSpec((1, n_pages, w), lambda b: (b, 0, 0))
    return pl.pallas_call(
        _past_decay_kernel, grid=(db,),
        in_specs=[per_b, pl.BlockSpec((w, w), const), pl.BlockSpec((w, w), const)], out_specs=per_b,
        out_shape=jax.ShapeDtypeStruct((db, n_pages, w), F32),
        compiler_params=_cparams(("arbitrary",)), name="past_decay",
    )(past_logf, within, total)


def _select_sample_kernel(pt_ref, iqm_ref, wcol_ref, iknew_ref, *refs, gs, topk, past, T):
    pages = refs[:gs]
    mp_ref, mn_ref, keys_scr, jstar_scr = refs[gs:]
    s_idx = pl.program_id(1)
    iqm = iqm_ref[0]
    wcol = wcol_ref[0]
    ncols = past + PAGE_SIZE

    def scores(kblk):
        r = jnp.dot(iqm, kblk.astype(BF16), preferred_element_type=F32)
        r = wcol * jnp.maximum(r, 0.0)
        acc = r[0:T]
        for h in range(1, IDX_HEADS):
            acc = acc + r[h * T:(h + 1) * T]
        return acc

    for g in range(gs):
        off = pl.multiple_of((s_idx * gs + g) * PAGE_SIZE, PAGE_SIZE)
        keys_scr[:, pl.ds(off, PAGE_SIZE)] = _sortable_key(scores(pages[g][0]))

    @pl.when(s_idx == pl.num_programs(1) - 1)
    def _():
        t_col = lax.broadcasted_iota(jnp.int32, (T, 1), 0)
        j_row = lax.broadcasted_iota(jnp.int32, (1, PAGE_SIZE), 1)
        keys_scr[:, past:ncols] = jnp.where(j_row <= t_col, _sortable_key(scores(iknew_ref[0])), INT_MIN)
        col = lax.broadcasted_iota(jnp.int32, (1, ncols), 1)

        def count(pred, *ops):
            return jnp.sum(pred(keys_scr[...], col, *ops).astype(jnp.int32), axis=1, keepdims=True)

        assert topk <= 2 * LANES
        gmax = []
        for par in range(2):
            g = jnp.full((T, LANES), INT_MIN, jnp.int32)
            for j in range(par, ncols // LANES, 2):
                g = jnp.maximum(g, keys_scr[:, j * LANES:(j + 1) * LANES])
            gmax.append(g)
        lo, hi = _group_bounds(jnp.concatenate(gmax, axis=1))
        thr, jstar = _topk_cut(count, topk, lo, hi, ncols, jstar_scr)
        sel = _selected(keys_scr[...], col, thr, jstar).astype(F32)
        for pg in range(past // PAGE_SIZE):
            mp_ref[0, pg] = sel[:, pg * PAGE_SIZE:(pg + 1) * PAGE_SIZE]
        mn_ref[0] = sel[:, past:]


def _select_sample(iq, iw, ik_new, cache_ik, pt_flat, n_pages, topk, gs):
    db, T, _ = iq.shape
    past = n_pages * PAGE_SIZE
    rows = IDX_HEADS * T
    iqm = jnp.swapaxes(iq.reshape(db, T, IDX_HEADS, IDX_DIM), 1, 2).reshape(db, rows, IDX_DIM).astype(BF16)
    wcol = jnp.swapaxes(iw, 1, 2).reshape(db, rows, 1)
    ik_pad = jnp.swapaxes(jnp.pad(ik_new, ((0, 0), (0, PAGE_SIZE - T), (0, 0))), 1, 2)
    per_b = lambda blk: pl.BlockSpec(blk, lambda b, s, pt: (b, 0, 0))
    grid_spec = pltpu.PrefetchScalarGridSpec(
        num_scalar_prefetch=1, grid=(db, n_pages // gs),
        in_specs=[per_b((1, rows, IDX_DIM)), per_b((1, rows, 1)), per_b((1, IDX_DIM, PAGE_SIZE))]
        + _page_specs((1, IDX_DIM, PAGE_SIZE), gs, n_pages),
        out_specs=(pl.BlockSpec((1, n_pages, T, PAGE_SIZE), lambda b, s, pt: (b, 0, 0, 0)), per_b((1, T, PAGE_SIZE))),
        scratch_shapes=[pltpu.VMEM((T, past + PAGE_SIZE), jnp.int32), pltpu.VMEM((T, LANES), jnp.int32)])
    return pl.pallas_call(
        functools.partial(_select_sample_kernel, gs=gs, topk=topk, past=past, T=T), grid_spec=grid_spec,
        out_shape=(jax.ShapeDtypeStruct((db, n_pages, T, PAGE_SIZE), F32),
                   jax.ShapeDtypeStruct((db, T, PAGE_SIZE), F32)),
        compiler_params=_cparams(("arbitrary", "arbitrary")), name="select_sample",
    )(pt_flat, iqm, wcol, ik_pad, *([cache_ik] * gs))


def _decode_kernel(pt_ref, q_ref, knew_ref, vnew_ref, gate_ref, *refs, mode, G, nheads, T):
    if mode == "fox":
        pd_ref, lfcol_ref, lfrow_ref = refs[:3]
        refs = refs[3:]
    else:
        mp_ref, mn_ref = refs[:2]
        refs = refs[2:]
    k_refs, v_refs = refs[:G], refs[G:2 * G]
    o_ref, m_scr, l_scr, acc_scr = refs[2 * G:2 * G + 4]
    s_idx = pl.program_id(1)
    rows = nheads * T
    vkeys = PAGE_SIZE * nheads
    nt = (((1,), (1,)), ((), ()))

    @pl.when(s_idx == 0)
    def _():
        m_scr[...] = jnp.full(m_scr.shape, MASK_NEG, F32)
        l_scr[...] = jnp.zeros(l_scr.shape, F32)
        acc_scr[...] = jnp.zeros(acc_scr.shape, F32)

    q = q_ref[0]
    row_head = lax.broadcasted_iota(jnp.int32, (rows, 1), 0) // T
    row_tok = lax.broadcasted_iota(jnp.int32, (rows, 1), 0) % T
    lane = lax.broadcasted_iota(jnp.int32, (1, vkeys), 1)
    own = jnp.where((lane % nheads) == row_head, 0.0, MASK_NEG)
    if mode == "fox":
        ri = lax.broadcasted_iota(jnp.int32, (rows, rows), 0)
        ci = lax.broadcasted_iota(jnp.int32, (rows, rows), 1)
        tri_col = jnp.logical_and(ri // T == ci // T, ci % T <= ri % T).astype(F32)
        nc_col = jnp.dot(tri_col, lfcol_ref[0], precision=lax.Precision.HIGHEST, preferred_element_type=F32)
        ui = lax.broadcasted_iota(jnp.int32, (LANES, LANES), 0)
        uj = lax.broadcasted_iota(jnp.int32, (LANES, LANES), 1)
        tri_row = jnp.logical_and(jnp.logical_and(ui % nheads == uj % nheads, ui <= uj), uj < rows).astype(F32)
        nc_row = jnp.dot(lfrow_ref[0], tri_row, precision=lax.Precision.HIGHEST, preferred_element_type=F32)
        own_nc = own + nc_col[:, 0:1]
    else:
        expand = (lax.broadcasted_iota(jnp.int32, (PAGE_SIZE, vkeys), 1) // nheads
                  == lax.broadcasted_iota(jnp.int32, (PAGE_SIZE, vkeys), 0)).astype(BF16)
        picked = jnp.dot(mp_ref[0].reshape(G * T, PAGE_SIZE).astype(BF16), expand, preferred_element_type=F32)

    def update(s, vals):
        width = vals[0].shape[0]
        m_prev = m_scr[:, 0:1]
        m_new = jnp.maximum(m_prev, jnp.max(s, axis=1, keepdims=True))
        p = jnp.exp(s - m_new)
        alpha = jnp.exp(m_prev - m_new)
        l_scr[...] = jnp.broadcast_to(alpha * l_scr[:, 0:1] + jnp.sum(p, axis=1, keepdims=True), l_scr.shape)
        p = p.astype(BF16)
        pv = jnp.dot(p[:, 0:width], vals[0], preferred_element_type=F32)
        for g in range(1, len(vals)):
            pv = pv + jnp.dot(p[:, g * width:(g + 1) * width], vals[g], preferred_element_type=F32)
        acc_scr[...] = alpha * acc_scr[...] + pv
        m_scr[...] = jnp.broadcast_to(m_new, m_scr.shape)

    parts = []
    for g in range(G):
        s = lax.dot_general(q, k_refs[g][0].astype(BF16), nt, preferred_element_type=F32) * ATTN_SCALE
        if mode == "fox":
            s = s + (own_nc + pd_ref[0, g:g + 1, :])
        else:
            keep = jnp.concatenate([picked[g * T:(g + 1) * T]] * nheads, axis=0) > 0.5
            s = jnp.where(keep, s + own, MASK_NEG)
        parts.append(s)
    update(jnp.concatenate(parts, axis=1), [v[0].astype(BF16) for v in v_refs])

    @pl.when(s_idx == pl.num_programs(1) - 1)
    def _():
        sn = lax.dot_general(q, knew_ref[0].astype(BF16), nt, preferred_element_type=F32) * ATTN_SCALE
        lane_n = lax.broadcasted_iota(jnp.int32, (1, LANES), 1)
        if mode == "fox":
            keep = (lane_n // nheads) <= row_tok
            sn = sn + (nc_col[:, 0:1] - nc_row[0:1, :])
        else:
            keep = jnp.concatenate([jnp.dot(mn_ref[0].astype(BF16), expand[:, 0:LANES],
                                            preferred_element_type=F32)] * nheads, axis=0) > 0.5
        sn = jnp.where(keep, sn + own[:, 0:LANES], MASK_NEG)
        update(sn, [vnew_ref[0].astype(BF16)])
        o = acc_scr[...] / l_scr[:, 0:1]
        for h in range(nheads):
            sl = slice(h * HEAD_DIM, (h + 1) * HEAD_DIM)
            g = gate_ref[0, :, sl]
            o_ref[0, :, sl] = o[h * T:(h + 1) * T] * (g / (1.0 + jnp.exp(-g)))


def _decode(mode, q, k_new, v_new, gate, extra, cache_k, cache_v, pt_flat, n_pages, G):
    db, T, w = q.shape
    nheads = w // HEAD_DIM
    rows = nheads * T
    assert rows <= LANES and nheads * HEAD_DIM == w
    vkeys = PAGE_SIZE * nheads
    q_rows = jnp.swapaxes(q.reshape(db, T, nheads, HEAD_DIM), 1, 2).reshape(db, rows, HEAD_DIM).astype(BF16)
    new_rows = lambda a: jnp.pad(a.reshape(db, rows, HEAD_DIM), ((0, 0), (0, LANES - rows), (0, 0)))
    per_b = lambda blk: pl.BlockSpec(blk, lambda b, s, pt: (b,) + (0,) * (len(blk) - 1))
    step = lambda blk: pl.BlockSpec(blk, lambda b, s, pt: (b, s) + (0,) * (len(blk) - 2))
    in_specs = [per_b((1, rows, HEAD_DIM)), per_b((1, LANES, HEAD_DIM)), per_b((1, LANES, HEAD_DIM)),
                per_b((1, T, w))]
    if mode == "fox":
        pd, logf_new = extra
        lf_col = jnp.broadcast_to(jnp.swapaxes(logf_new, 1, 2).reshape(db, rows, 1), (db, rows, LANES))
        lf_row = jnp.broadcast_to(jnp.pad(logf_new.reshape(db, 1, rows), ((0, 0), (0, 0), (0, LANES - rows))),
                                  (db, 8, LANES))
        in_specs += [step((1, G, vkeys)), per_b((1, rows, LANES)), per_b((1, 8, LANES))]
        args = (pd, lf_col, lf_row)
    else:
        mask_past, mask_new = extra
        in_specs += [step((1, G, T, PAGE_SIZE)), per_b((1, T, PAGE_SIZE))]
        args = (mask_past, mask_new)
    in_specs += _page_specs((1, vkeys, HEAD_DIM), G, n_pages) + _page_specs((1, vkeys, HEAD_DIM), G, n_pages)
    grid_spec = pltpu.PrefetchScalarGridSpec(
        num_scalar_prefetch=1, grid=(db, n_pages // G), in_specs=in_specs, out_specs=per_b((1, T, w)),
        scratch_shapes=[pltpu.VMEM((rows, LANES), F32), pltpu.VMEM((rows, LANES), F32),
                        pltpu.VMEM((rows, HEAD_DIM), F32)])
    return pl.pallas_call(
        functools.partial(_decode_kernel, mode=mode, G=G, nheads=nheads, T=T), grid_spec=grid_spec,
        out_shape=jax.ShapeDtypeStruct((db, T, w), F32),
        compiler_params=_cparams(("arbitrary", "arbitrary")), name=f"decode_{mode}",
    )(pt_flat, q_rows, new_rows(k_new), new_rows(v_new), gate, *args, *([cache_k] * G), *([cache_v] * G))


def _sample_group(x, caches, page_table, w_in, b_f, w_out, ln_g, ln_b):
    cache_fox_k, cache_fox_v, cache_fox_logf, cache_dsa_k, cache_dsa_v, cache_idx_k = caches
    db, T, d = x.shape
    n_pages = page_table.shape[1]
    past = n_pages * PAGE_SIZE
    nphys = cache_fox_k.shape[0]
    G = min(16, n_pages)
    pt_flat = page_table.reshape(-1).astype(jnp.int32)
    x2 = x.reshape(db * T, d)
    pos = jnp.tile(past + jnp.arange(T), db)
    fq, fk, fv, logf, fg, dq, dk, dv, dg, iq, ik, iw = _project(x2, pos, w_in, b_f, db * T)
    r3 = lambda a: a.reshape(db, T, a.shape[-1])
    kv_rows = lambda c: c.reshape(nphys, PAGE_SIZE * c.shape[3], HEAD_DIM)
    past_logf = _gather_rows(cache_fox_logf.reshape(nphys, 1, PAGE_SIZE * FOX_HEADS), pt_flat, db, n_pages,
                             min(32, n_pages))
    pd = _past_decay(past_logf)
    mix_fox = _decode("fox", r3(fq), r3(fk), r3(fv), r3(fg), (pd, r3(logf)),
                      kv_rows(cache_fox_k), kv_rows(cache_fox_v), pt_flat, n_pages, G)
    topk = min(TOPK_MAX, (past + T) // 4)
    ik_pages = jnp.swapaxes(cache_idx_k.reshape(nphys, PAGE_SIZE, IDX_DIM), 1, 2)
    masks = _select_sample(r3(iq), r3(iw), r3(ik), ik_pages, pt_flat, n_pages, topk, min(32, n_pages))
    mix_dsa = _decode("dsa", r3(dq), r3(dk), r3(dv), r3(dg), masks,
                      kv_rows(cache_dsa_k), kv_rows(cache_dsa_v), pt_flat, n_pages, G)
    y = _output_and_norm(x2, mix_fox.reshape(db * T, WIDTH), mix_dsa.reshape(db * T, WIDTH), w_out, ln_g, ln_b,
                         db * T)
    heads = lambda a, h: a.reshape(db, 1, T, h, HEAD_DIM)
    return (y.reshape(db, T, d), heads(fk, FOX_HEADS), heads(fv, FOX_HEADS), logf.reshape(db, 1, T, FOX_HEADS),
            heads(dk, DSA_HEADS), heads(dv, DSA_HEADS), ik.reshape(db, 1, T, IDX_DIM))


def kernel(x_prompt, x_sample, cache_fox_k, cache_fox_v, cache_fox_logf, cache_dsa_k, cache_dsa_v, cache_idx_k,
           page_table, w_in, b_f, w_out, ln_g, ln_b):
    assert w_in.shape[0] == DEPTH and cache_fox_k.shape[1] == DEPTH
    params = (w_in[0], b_f[0], w_out[0], ln_g[0], ln_b[0])
    caches = (cache_fox_k, cache_fox_v, cache_fox_logf, cache_dsa_k, cache_dsa_v, cache_idx_k)
    p = _prompt_group(x_prompt, *params)
    s = _sample_group(x_sample, caches, page_table, *params)
    return (p[0], s[0], *p[1:], *s[1:])
```
